```python
import jax
import jax.numpy as jnp
from jax import lax
import numpy as np

D_MODEL = 1024
BATCH = 4
SEQ = 4096
DEPTH = 2
DEC_BATCH = 32
DEC_SEQ = 1
PAST_LEN = 8192
PAGE_SIZE = 128

N_EVEN = (DEPTH + 1) // 2
N_ODD = DEPTH // 2

POOL_WIDTH = D_MODEL // 2
POOL_GROUPS = 4
POOL_GROUP_DIM = POOL_WIDTH // POOL_GROUPS
POOL_WINDOWS = (2, 4, 8, 16)
POOL_STATE = max(POOL_WINDOWS) - 1

RET_HEADS = 4
RET_QK_DIM = 64
RET_V_DIM = (D_MODEL - POOL_WIDTH) // RET_HEADS
RET_CHUNK = 128
ROPE_BASE = 10000.0
RET_QK = RET_HEADS * RET_QK_DIM
RET_V = RET_HEADS * RET_V_DIM
AB_IN = POOL_WIDTH + 2 * RET_QK + 2 * RET_V

FOX_HEADS = 16
FOX_HEAD_DIM = D_MODEL // FOX_HEADS
FOX_BLOCK = 128
FOX_IN = 3 * D_MODEL + FOX_HEADS

N_MEM = 256
MEM_HEADS = 4
MEM_HEAD_DIM = D_MODEL // MEM_HEADS

N_GROUPS = 4
EXPERTS_PER_GROUP = 4
N_EXPERTS = N_GROUPS * EXPERTS_PER_GROUP
EXPERT_HIDDEN = D_MODEL // 4
TOP_K_FINE = 2

DEEPNORM_ALPHA = (2 * DEPTH) ** 0.25
DEEPNORM_BETA = (8 * DEPTH) ** -0.25
LN_EPS = 1e-5

kernel_name = 'hybrid_pool_retention_fox_hmoe_step'


def layer_norm(x, g, b):
    xf = x.astype(jnp.float32)
    mu = jnp.mean(xf, axis=-1, keepdims=True)
    var = jnp.mean(jnp.square(xf - mu), axis=-1, keepdims=True)
    return ((xf - mu) * lax.rsqrt(var + LN_EPS)).astype(x.dtype) * g + b


def post_norm(x, y, g, b):
    return layer_norm(DEEPNORM_ALPHA * x + y, g, b)


def head_norm(o, g):
    B, T, H, dv = o.shape
    of = o.astype(jnp.float32)
    mu = jnp.mean(of, axis=-1, keepdims=True)
    var = jnp.mean(jnp.square(of - mu), axis=-1, keepdims=True)
    on = ((of - mu) * lax.rsqrt(var + LN_EPS)).astype(o.dtype)
    return on.reshape(B, T, H * dv) * g


def rotary(x, pos):
    half = x.shape[-1] // 2
    inv = ROPE_BASE ** (-jnp.arange(half, dtype=jnp.float32) / half)
    ang = pos.astype(jnp.float32)[:, None] * inv[None, :]
    cos = jnp.cos(ang)[None, :, None, :].astype(x.dtype)
    sin = jnp.sin(ang)[None, :, None, :].astype(x.dtype)
    x1, x2 = x[..., :half], x[..., half:]
    return jnp.concatenate([x1 * cos - x2 * sin, x1 * sin + x2 * cos], axis=-1)


def retention_log_decay():
    return jnp.log(1.0 - 2.0 ** (-5.0 - jnp.arange(RET_HEADS, dtype=jnp.float32)))


def pool_mixer(u_ext, n_prev, pos0, w_pool, s_pool):
    B, L, _ = u_ext.shape
    T = L - n_prev
    pad = max(POOL_WINDOWS)
    z = jnp.concatenate([jnp.zeros((B, pad, POOL_WIDTH), u_ext.dtype), u_ext], axis=1)
    cs = jnp.cumsum(z.astype(jnp.float32), axis=1)
    pos = pos0 + jnp.arange(T)
    start = pad + n_prev
    outs = []
    for g, w in enumerate(POOL_WINDOWS):
        lo, hi = g * POOL_GROUP_DIM, (g + 1) * POOL_GROUP_DIM
        win = cs[:, start:start + T, lo:hi] - cs[:, start - w:start - w + T, lo:hi]
        cnt = jnp.minimum(w, pos + 1).astype(jnp.float32)[None, :, None]
        outs.append(win / cnt)
    pooled = jnp.concatenate(outs, axis=-1).astype(u_ext.dtype)
    diff = (pooled - u_ext[:, n_prev:]).reshape(B, T, POOL_GROUPS, POOL_GROUP_DIM)
    mixed = jnp.einsum('btgc,gcd->btgd', diff, w_pool).reshape(B, T, POOL_WIDTH)
    return mixed * s_pool


def retention_chunk(q, k, v, s_prev, log_gamma):
    L = q.shape[2]
    i = jnp.arange(L, dtype=jnp.float32)
    lg = log_gamma[:, None]
    dist = i[:, None] - i[None, :]
    decay = jnp.where(dist >= 0, jnp.exp(lg[:, :, None] * jnp.maximum(dist, 0.0)), 0.0)
    scores = jnp.einsum('bhid,bhjd->bhij', q, k) * decay.astype(q.dtype)
    o = jnp.einsum('bhij,bhje->bhie', scores, v)
    q_dec = q * jnp.exp(lg * (i + 1.0))[None, :, :, None].astype(q.dtype)
    o = o + jnp.einsum('bhid,bhde->bhie', q_dec, s_prev.astype(q.dtype))
    k_dec = k * jnp.exp(lg * (L - 1.0 - i))[None, :, :, None].astype(k.dtype)
    s_new = s_prev.astype(jnp.float32) * jnp.exp(lg * L).reshape(1, RET_HEADS, 1, 1) \
        + jnp.einsum('bhjd,bhje->bhde', k_dec, v).astype(jnp.float32)
    return o, s_new


def retention_prompt(q, k, v, log_gamma):
    B, H, S, dk = q.shape
    dv = v.shape[-1]
    nc = S // RET_CHUNK

    def to_chunks(a):
        return a.reshape(B, H, nc, RET_CHUNK, a.shape[-1]).transpose(2, 0, 1, 3, 4)

    def step(s_carry, qkv):
        qc, kc, vc = qkv
        o, s_next = retention_chunk(qc, kc, vc, s_carry, log_gamma)
        return s_next, o

    s0 = jnp.zeros((B, H, dk, dv), jnp.float32)
    s_fin, o = lax.scan(step, s0, (to_chunks(q), to_chunks(k), to_chunks(v)))
    o = o.transpose(1, 2, 0, 3, 4).reshape(B, H, S, dv)
    return o, s_fin


def ab_mixer(x, pool_prev, ret_prev, pos0, w_in, w_pool, s_pool, gn_g, w_o):
    B, T, _ = x.shape
    h = x @ w_in
    u, q, k, v, g = jnp.split(
        h, [POOL_WIDTH, POOL_WIDTH + RET_QK, POOL_WIDTH + 2 * RET_QK, POOL_WIDTH + 2 * RET_QK + RET_V], axis=-1)
    u_ext = jnp.concatenate([pool_prev.astype(u.dtype), u], axis=1)
    a_out = pool_mixer(u_ext, pool_prev.shape[1], pos0, w_pool, s_pool)
    pos = pos0 + jnp.arange(T)
    q = rotary(q.reshape(B, T, RET_HEADS, RET_QK_DIM), pos).transpose(0, 2, 1, 3)
    k = (rotary(k.reshape(B, T, RET_HEADS, RET_QK_DIM), pos) * RET_QK_DIM ** -0.5).transpose(0, 2, 1, 3)
    v = v.reshape(B, T, RET_HEADS, RET_V_DIM).transpose(0, 2, 1, 3)
    log_gamma = retention_log_decay()
    if ret_prev is None:
        o, s_new = retention_prompt(q, k, v, log_gamma)
    else:
        o, s_new = retention_chunk(q, k, v, ret_prev, log_gamma)
    o = head_norm(o.transpose(0, 2, 1, 3), gn_g)
    b_out = jax.nn.silu(g) * o
    y = jnp.concatenate([a_out, b_out], axis=-1) @ w_o
    return y, u_ext[:, -POOL_STATE:], s_new


def fox_project(x, w_in, b_f):
    B, T, _ = x.shape
    h = x @ w_in
    q, k, v, fl = jnp.split(h, [D_MODEL, 2 * D_MODEL, 3 * D_MODEL], axis=-1)
    q = q.reshape(B, T, FOX_HEADS, FOX_HEAD_DIM)
    k = k.reshape(B, T, FOX_HEADS, FOX_HEAD_DIM)
    v = v.reshape(B, T, FOX_HEADS, FOX_HEAD_DIM)
    logf = jax.nn.log_sigmoid((fl + b_f).astype(jnp.float32))
    return q, k, v, logf


def fox_prompt(q, k, v, logf):
    B, S, H, Dh = q.shape
    scale = Dh ** -0.5
    c = jnp.cumsum(logf, axis=1).transpose(0, 2, 1)
    kpos = jnp.arange(S)

    def block(bi):
        s0 = bi * FOX_BLOCK
        qb = lax.dynamic_slice_in_dim(q, s0, FOX_BLOCK, axis=1)
        cb = lax.dynamic_slice_in_dim(c, s0, FOX_BLOCK, axis=2)
        logits = jnp.einsum('bqhd,bkhd->bhqk', qb, k).astype(jnp.float32) * scale
        logits = logits + (cb[..., :, None] - c[..., None, :])
        qpos = s0 + jnp.arange(FOX_BLOCK)
        logits = jnp.where(kpos[None, :] <= qpos[:, None], logits, -jnp.inf)
        p = jax.nn.softmax(logits, axis=-1).astype(v.dtype)
        return jnp.einsum('bhqk,bkhd->bqhd', p, v)

    o = lax.map(block, jnp.arange(S // FOX_BLOCK))
    return o.transpose(1, 0, 2, 3, 4).reshape(B, S, H * Dh)


def fox_sample(q, k, v, logf, k_past, v_past, logf_past):
    B, T, H, Dh = q.shape
    P = k_past.shape[1]
    scale = Dh ** -0.5
    c = jnp.cumsum(jnp.concatenate([logf_past.astype(jnp.float32), logf], axis=1), axis=1)
    ck = c.transpose(0, 2, 1)
    cq = ck[:, :, P:]
    lp = jnp.einsum('bqhd,bkhd->bhqk', q, k_past)
    ln = jnp.einsum('bqhd,bkhd->bhqk', q, k)
    logits = jnp.concatenate([lp, ln], axis=-1).astype(jnp.float32) * scale
    logits = logits + (cq[..., :, None] - ck[..., None, :])
    kpos = jnp.arange(P + T)
    qpos = P + jnp.arange(T)
    logits = jnp.where(kpos[None, :] <= qpos[:, None], logits, -jnp.inf)
    p = jax.nn.softmax(logits, axis=-1).astype(v.dtype)
    o = jnp.einsum('bhqk,bkhd->bqhd', p[..., :P], v_past) + jnp.einsum('bhqk,bkhd->bqhd', p[..., P:], v)
    return o.reshape(B, T, H * Dh)


def mem_kv(mem, w_kv):
    B, M, _ = mem.shape
    k, v = jnp.split(mem @ w_kv, 2, axis=-1)
    return k.reshape(B, M, MEM_HEADS, MEM_HEAD_DIM), v.reshape(B, M, MEM_HEADS, MEM_HEAD_DIM)


def mem_attend(x, k, v, w_q, w_o):
    B, T, _ = x.shape
    q = (x @ w_q).reshape(B, T, MEM_HEADS, MEM_HEAD_DIM)
    logits = jnp.einsum('bqhd,bkhd->bhqk', q, k).astype(jnp.float32) * MEM_HEAD_DIM ** -0.5
    p = jax.nn.softmax(logits, axis=-1).astype(v.dtype)
    o = jnp.einsum('bhqk,bkhd->bqhd', p, v).reshape(B, T, D_MODEL)
    return o @ w_o


def hier_moe(x, w_gc, b_gc, w_gf, b_gf, w_up, w_gate, w_down):
    B, T, D = x.shape
    xt = x.reshape(B * T, D)
    n = xt.shape[0]
    coarse = (xt @ w_gc + b_gc).astype(jnp.float32)
    p_coarse = jax.nn.softmax(coarse, axis=-1)
    p_grp, grp = lax.top_k(p_coarse, 1)
    fine = (jnp.einsum('nd,gde->nge', xt, w_gf) + b_gf).astype(jnp.float32)
    idx = jnp.broadcast_to(grp[:, :, None], (n, 1, EXPERTS_PER_GROUP))
    fine_sel = jnp.take_along_axis(fine, idx, axis=1)[:, 0]
    top_val, top_idx = lax.top_k(fine_sel, TOP_K_FINE)
    w_sel = jax.nn.softmax(top_val, axis=-1) * p_grp
    expert_id = grp * EXPERTS_PER_GROUP + top_idx
    gates = jnp.zeros((n, N_EXPERTS), jnp.float32).at[jnp.arange(n)[:, None], expert_id].add(w_sel)
    h = jax.nn.silu(jnp.einsum('nd,edh->neh', xt, w_gate)) * jnp.einsum('nd,edh->neh', xt, w_up)
    y = jnp.einsum('neh,ehd->nd', h * gates[:, :, None].astype(h.dtype), w_down)
    return y.reshape(B, T, D)


def setup_inputs(seed: int = 0) -> dict:
    key = jax.random.key(seed)
    keys = list(jax.random.split(key, 48))

    def nrm(i, shape, scale=1.0):
        return jax.random.normal(keys[i], shape, jnp.float32) * scale

    n_pages = PAST_LEN // PAGE_SIZE
    n_used = DEC_BATCH * n_pages
    n_phys = (5 * n_used) // 4
    d_is = D_MODEL ** -0.5
    beta = DEEPNORM_BETA
    fox_bias = jnp.linspace(2.0, 7.0, FOX_HEADS, dtype=jnp.float32)

    x_prompt = nrm(0, (BATCH, SEQ, D_MODEL))
    x_sample = nrm(1, (DEC_BATCH, DEC_SEQ, D_MODEL))
    mem_prompt = nrm(2, (BATCH, N_MEM, D_MODEL))
    state_pool = nrm(3, (N_EVEN, DEC_BATCH, POOL_STATE, POOL_WIDTH))
    state_ret = nrm(4, (N_EVEN, DEC_BATCH, RET_HEADS, RET_QK_DIM, RET_V_DIM), 0.3)
    cache_fox_k = nrm(5, (N_ODD, n_phys, PAGE_SIZE, FOX_HEADS, FOX_HEAD_DIM))
    cache_fox_v = nrm(6, (N_ODD, n_phys, PAGE_SIZE, FOX_HEADS, FOX_HEAD_DIM), 0.5)
    cache_fox_logf = jax.nn.log_sigmoid(nrm(7, (N_ODD, n_phys, PAGE_SIZE, FOX_HEADS), 0.5) + fox_bias)
    cache_mem_k = nrm(8, (DEPTH, DEC_BATCH, N_MEM, MEM_HEADS, MEM_HEAD_DIM))
    cache_mem_v = nrm(9, (DEPTH, DEC_BATCH, N_MEM, MEM_HEADS, MEM_HEAD_DIM), 0.5)
    page_table = jax.random.permutation(keys[10], n_phys)[:n_used].reshape(DEC_BATCH, n_pages).astype(jnp.int32)

    ln_g = 1.0 + nrm(11, (DEPTH, 3, D_MODEL), 0.02)
    ln_b = nrm(12, (DEPTH, 3, D_MODEL), 0.02)
    ab_w_in = jnp.concatenate([
        nrm(13, (N_EVEN, D_MODEL, POOL_WIDTH + 2 * RET_QK), d_is),
        nrm(14, (N_EVEN, D_MODEL, RET_V), d_is * beta),
        nrm(15, (N_EVEN, D_MODEL, RET_V), d_is)], axis=-1)
    pool_w = nrm(16, (N_EVEN, POOL_GROUPS, POOL_GROUP_DIM, POOL_GROUP_DIM), POOL_GROUP_DIM ** -0.5)
    pool_scale = 1.0 + nrm(17, (N_EVEN, POOL_WIDTH), 0.02)
    ret_gn_g = 1.0 + nrm(18, (N_EVEN, RET_V), 0.02)
    ab_w_o = nrm(19, (N_EVEN, D_MODEL, D_MODEL), d_is * beta)
    fox_w_in = jnp.concatenate([
        nrm(20, (N_ODD, D_MODEL, 2 * D_MODEL), d_is),
        nrm(21, (N_ODD, D_MODEL, D_MODEL), d_is * beta),
        nrm(22, (N_ODD, D_MODEL, FOX_HEADS), d_is)], axis=-1)
    fox_b_f = fox_bias[None, :] + nrm(23, (N_ODD, FOX_HEADS), 0.1)
    fox_w_o = nrm(24, (N_ODD, D_MODEL, D_MODEL), d_is * beta)
    mem_wq = nrm(25, (DEPTH, D_MODEL, D_MODEL), d_is)
    mem_wkv = jnp.concatenate([
        nrm(26, (DEPTH, D_MODEL, D_MODEL), d_is),
        nrm(27, (DEPTH, D_MODEL, D_MODEL), d_is * beta)], axis=-1)
    mem_wo = nrm(28, (DEPTH, D_MODEL, D_MODEL), d_is * beta)
    moe_w_gc = nrm(29, (DEPTH, D_MODEL, N_GROUPS), d_is)
    moe_b_gc = nrm(30, (DEPTH, N_GROUPS), 0.01)
    moe_w_gf = nrm(31, (DEPTH, N_GROUPS, D_MODEL, EXPERTS_PER_GROUP), d_is)
    moe_b_gf = nrm(32, (DEPTH, N_GROUPS, EXPERTS_PER_GROUP), 0.01)
    moe_w_up = nrm(33, (DEPTH, N_EXPERTS, D_MODEL, EXPERT_HIDDEN), d_is * beta)
    moe_w_gate = nrm(34, (DEPTH, N_EXPERTS, D_MODEL, EXPERT_HIDDEN), d_is)
    moe_w_down = nrm(35, (DEPTH, N_EXPERTS, EXPERT_HIDDEN, D_MODEL), EXPERT_HIDDEN ** -0.5 * beta)
    return {
        'x_prompt': x_prompt, 'x_sample': x_sample, 'mem_prompt': mem_prompt,
        'state_pool': state_pool, 'state_ret': state_ret,
        'cache_fox_k': cache_fox_k, 'cache_fox_v': cache_fox_v, 'cache_fox_logf': cache_fox_logf,
        'cache_mem_k': cache_mem_k, 'cache_mem_v': cache_mem_v, 'page_table': page_table,
        'ln_g': ln_g, 'ln_b': ln_b,
        'ab_w_in': ab_w_in, 'pool_w': pool_w, 'pool_scale': pool_scale, 'ret_gn_g': ret_gn_g, 'ab_w_o': ab_w_o,
        'fox_w_in': fox_w_in, 'fox_b_f': fox_b_f, 'fox_w_o': fox_w_o,
        'mem_wq': mem_wq, 'mem_wkv': mem_wkv, 'mem_wo': mem_wo,
        'moe_w_gc': moe_w_gc, 'moe_b_gc': moe_b_gc, 'moe_w_gf': moe_w_gf, 'moe_b_gf': moe_b_gf,
        'moe_w_up': moe_w_up, 'moe_w_gate': moe_w_gate, 'moe_w_down': moe_w_down,
    }


def reference(x_prompt, x_sample, mem_prompt, state_pool, state_ret, cache_fox_k, cache_fox_v,
              cache_fox_logf, cache_mem_k, cache_mem_v, page_table, ln_g, ln_b, ab_w_in, pool_w,
              pool_scale, ret_gn_g, ab_w_o, fox_w_in, fox_b_f, fox_w_o, mem_wq, mem_wkv, mem_wo,
              moe_w_gc, moe_b_gc, moe_w_gf, moe_b_gf, moe_w_up, moe_w_gate, moe_w_down):
    yp, ys = x_prompt, x_sample
    bp, bs = x_prompt.shape[0], x_sample.shape[0]
    pool_p, pool_s, ret_p, ret_s = [], [], [], []
    fk_p, fk_s, fv_p, fv_s, fl_p, fl_s = [], [], [], [], [], []
    mk_p, mv_p = [], []
    for l in range(DEPTH):
        if l % 2 == 0:
            e = l // 2
            mix_p, up, sp = ab_mixer(yp, jnp.zeros((bp, 0, POOL_WIDTH), yp.dtype), None, 0,
                                     ab_w_in[e], pool_w[e], pool_scale[e], ret_gn_g[e], ab_w_o[e])
            mix_s, us, ss = ab_mixer(ys, state_pool[e], state_ret[e], PAST_LEN,
                                     ab_w_in[e], pool_w[e], pool_scale[e], ret_gn_g[e], ab_w_o[e])
            pool_p.append(up)
            pool_s.append(us)
            ret_p.append(sp)
            ret_s.append(ss)
        else:
            o = l // 2
            qp, kp, vp, lfp = fox_project(yp, fox_w_in[o], fox_b_f[o])
            mix_p = fox_prompt(qp, kp, vp, lfp) @ fox_w_o[o]
            qs, kss, vss, lfs = fox_project(ys, fox_w_in[o], fox_b_f[o])
            k_past = cache_fox_k[o, page_table].reshape(bs, -1, FOX_HEADS, FOX_HEAD_DIM)
            v_past = cache_fox_v[o, page_table].reshape(bs, -1, FOX_HEADS, FOX_HEAD_DIM)
            lf_past = cache_fox_logf[o, page_table].reshape(bs, -1, FOX_HEADS)
            mix_s = fox_sample(qs, kss, vss, lfs, k_past, v_past, lf_past) @ fox_w_o[o]
            fk_p.append(kp)
            fv_p.append(vp)
            fl_p.append(lfp)
            fk_s.append(kss)
            fv_s.append(vss)
            fl_s.append(lfs)
        yp = post_norm(yp, mix_p, ln_g[l, 0], ln_b[l, 0])
        ys = post_norm(ys, mix_s, ln_g[l, 0], ln_b[l, 0])
        mk, mv = mem_kv(mem_prompt, mem_wkv[l])
        mk_p.append(mk)
        mv_p.append(mv)
        yp = post_norm(yp, mem_attend(yp, mk, mv, mem_wq[l], mem_wo[l]), ln_g[l, 1], ln_b[l, 1])
        ys = post_norm(ys, mem_attend(ys, cache_mem_k[l], cache_mem_v[l], mem_wq[l], mem_wo[l]),
                       ln_g[l, 1], ln_b[l, 1])
        yp = post_norm(yp, hier_moe(yp, moe_w_gc[l], moe_b_gc[l], moe_w_gf[l], moe_b_gf[l],
                                    moe_w_up[l], moe_w_gate[l], moe_w_down[l]), ln_g[l, 2], ln_b[l, 2])
        ys = post_norm(ys, hier_moe(ys, moe_w_gc[l], moe_b_gc[l], moe_w_gf[l], moe_b_gf[l],
                                    moe_w_up[l], moe_w_gate[l], moe_w_down[l]), ln_g[l, 2], ln_b[l, 2])
    new_pool_p = jnp.stack(pool_p)
    new_pool_s = jnp.stack(pool_s)
    new_ret_p = jnp.stack(ret_p)
    new_ret_s = jnp.stack(ret_s)
    new_fk_p = jnp.stack(fk_p)
    new_fk_s = jnp.stack(fk_s)
    new_fv_p = jnp.stack(fv_p)
    new_fv_s = jnp.stack(fv_s)
    new_fl_p = jnp.stack(fl_p)
    new_fl_s = jnp.stack(fl_s)
    new_mk_p = jnp.stack(mk_p)
    new_mv_p = jnp.stack(mv_p)
    return (yp, ys, new_pool_p, new_pool_s, new_ret_p, new_ret_s, new_fk_p, new_fk_s,
            new_fv_p, new_fv_s, new_fl_p, new_fl_s, new_mk_p, new_mv_p)
```

```python
import functools

import numpy as np
import jax
import jax.numpy as jnp
from jax import lax
from jax.experimental import pallas as pl
from jax.experimental.pallas import tpu as pltpu

F32 = jnp.float32
BF16 = jnp.bfloat16

D_MODEL = 1024
DEPTH = 2
PAST_LEN = 8192
PAGE_SIZE = 128

POOL_WIDTH = 512
POOL_GROUP_DIM = 128
POOL_WINDOWS = (2, 4, 8, 16)
POOL_STATE = 15
POOL_HIST = 16

RET_HEADS = 4
RET_QK_DIM = 64
RET_V_DIM = 128
RET_QK = 256
RET_V = 512
RET_CHUNK = 128
ROPE_BASE = 10000.0
ROPE_HALF = RET_QK_DIM // 2

FOX_HEADS = 16
FOX_HEAD_DIM = 64

N_MEM = 256
MEM_HEADS = 4
MEM_HEAD_DIM = 256

N_GROUPS = 4
EXPERTS_PER_GROUP = 4
N_EXPERTS = 16
EXPERT_HIDDEN = 256
ROUTER_LANES = 128

DEEPNORM_ALPHA = (2 * DEPTH) ** 0.25
LN_EPS = 1e-5

VMEM_LIMIT = 56 * 1024 * 1024


def _cparams(*sem):
    return pltpu.CompilerParams(dimension_semantics=sem, vmem_limit_bytes=VMEM_LIMIT)


def _full(shape):
    n = len(shape)
    return pl.BlockSpec(shape, lambda *_: (0,) * n)


def _ln(z, g, b):
    mu = jnp.mean(z, axis=-1, keepdims=True)
    zc = z - mu
    var = jnp.mean(zc * zc, axis=-1, keepdims=True)
    return zc * lax.rsqrt(var + LN_EPS) * g + b


def _dot(a, b):
    return jnp.dot(a, b, preferred_element_type=F32)


def _dot_nt(a, b):
    return lax.dot_general(a, b, (((1,), (1,)), ((), ())), preferred_element_type=F32)


def _silu(x):
    return x * jax.nn.sigmoid(x)


def _qk_perm():
    p = np.arange(RET_QK)
    h = (p % 128) // ROPE_HALF
    d = (p % ROPE_HALF) + ROPE_HALF * (p // 128)
    return h * RET_QK_DIM + d


def _lane_head():
    return (np.arange(RET_QK) % 128) // ROPE_HALF


def _log_gamma():
    return np.log(1.0 - 2.0 ** (-5.0 - np.arange(RET_HEADS, dtype=np.float64)))


def _rope_tables(pos):
    inv = (np.float32(ROPE_BASE) ** (-np.arange(ROPE_HALF, dtype=np.float32) / np.float32(ROPE_HALF))).astype(np.float32)
    ang = pos.astype(np.float32)[:, None] * inv[None, :]
    cos = np.tile(np.cos(ang).astype(np.float32), (1, RET_HEADS))
    sin = np.tile(np.sin(ang).astype(np.float32), (1, RET_HEADS))
    return cos, sin


def _ret_tables(chunk):
    lg = _log_gamma()
    lh = _lane_head()
    i = np.arange(chunk, dtype=np.float64)
    dist = i[:, None] - i[None, :]
    dmask = np.where(dist >= 0, np.exp(lg[:, None, None] * np.maximum(dist, 0.0)), 0.0)
    qdec = np.exp(lg[lh][None, :] * (i[:, None] + 1.0))
    kdec = np.exp(lg[lh][None, :] * (chunk - 1.0 - i[:, None]))
    sdec = np.repeat(np.exp(lg[lh] * chunk)[:, None], RET_V, axis=1)
    bmask = (lh[:, None] == (np.arange(RET_V) // RET_V_DIM)[None, :]).astype(np.float64)
    hmask = (lh[None, :] == np.arange(RET_HEADS)[:, None]).astype(np.float64)
    f = lambda a: jnp.asarray(a.astype(np.float32))
    return f(dmask), f(qdec), f(kdec), f(sdec), f(bmask), f(hmask)


def _rotary_cat(x, cos, sin):
    x1, x2 = x[:, :128], x[:, 128:]
    return jnp.concatenate([x1 * cos - x2 * sin, x1 * sin + x2 * cos], axis=1)


def _head_norm(o, gn):
    parts = []
    for h in range(RET_HEADS):
        oh = o[:, h * RET_V_DIM:(h + 1) * RET_V_DIM]
        mu = jnp.mean(oh, axis=-1, keepdims=True)
        oc = oh - mu
        var = jnp.mean(oc * oc, axis=-1, keepdims=True)
        parts.append(oc * lax.rsqrt(var + LN_EPS))
    return jnp.concatenate(parts, axis=1) * gn


def _ab_prompt_kernel(tm, x_ref, win_ref, cos_ref, sin_ref, dmask_ref, qdec_ref, kdec_ref, sdec_ref, bmask_ref,
                      hmask_ref, wpool_ref, spool_ref, gn_ref, wo_ref, lng_ref, lnb_ref,
                      y_ref, pool_ref, ret_ref, uext, state, cat):
    j = pl.program_id(1)

    @pl.when(j == 0)
    def _():
        uext[0:POOL_HIST, :] = jnp.zeros((POOL_HIST, POOL_WIDTH), F32)
        state[...] = jnp.zeros(state.shape, F32)

    x = x_ref[0]
    h = _dot(x.astype(BF16), win_ref[...])
    u = h[:, :POOL_WIDTH]
    uext[POOL_HIST:POOL_HIST + tm, :] = u

    pos = j * tm + lax.broadcasted_iota(jnp.int32, (tm, 1), 0)
    for g, w in enumerate(POOL_WINDOWS):
        lanes = slice(g * POOL_GROUP_DIM, (g + 1) * POOL_GROUP_DIM)
        win = u[:, lanes]
        for s in range(1, w):
            win = win + uext[POOL_HIST - s:POOL_HIST - s + tm, lanes]
        cnt = jnp.minimum(w, pos + 1).astype(F32)
        diff = win / cnt - u[:, lanes]
        mixed = _dot(diff.astype(BF16), wpool_ref[g]) * spool_ref[:, lanes]
        cat[:, lanes] = mixed.astype(BF16)
    tail = uext[tm:tm + POOL_HIST, :]
    uext[0:POOL_HIST, :] = tail
    pool_ref[0] = tail

    cos = cos_ref[...]
    sin = sin_ref[...]
    qr = _rotary_cat(h[:, POOL_WIDTH:POOL_WIDTH + RET_QK], cos, sin)
    kr = _rotary_cat(h[:, POOL_WIDTH + RET_QK:POOL_WIDTH + 2 * RET_QK], cos, sin) * (RET_QK_DIM ** -0.5)
    v = h[:, POOL_WIDTH + 2 * RET_QK:POOL_WIDTH + 2 * RET_QK + RET_V]
    gate = h[:, POOL_WIDTH + 2 * RET_QK + RET_V:]
    gn = gn_ref[...]
    c = RET_CHUNK
    for ci in range(tm // c):
        rows = slice(ci * c, (ci + 1) * c)
        qc, kc = qr[rows], kr[rows]
        kcb = kc.astype(BF16)
        vcb = v[rows].astype(BF16)
        sm = state[...]
        o = _dot((qc * qdec_ref[...]).astype(BF16), sm.astype(BF16))
        parts = []
        for hh in range(RET_HEADS):
            qm = (qc * hmask_ref[hh:hh + 1, :]).astype(BF16)
            sc = _dot_nt(qm, kcb) * dmask_ref[hh]
            parts.append(_dot(sc.astype(BF16), vcb[:, hh * RET_V_DIM:(hh + 1) * RET_V_DIM]))
        o = o + jnp.concatenate(parts, axis=1)
        kdt = (kc * kdec_ref[...]).T.astype(BF16)
        state[...] = sm * sdec_ref[...] + _dot(kdt, vcb) * bmask_ref[...]
        cat[rows, POOL_WIDTH:] = (_silu(gate[rows]) * _head_norm(o, gn)).astype(BF16)
    ret_ref[0] = state[...]

    mix = _dot(cat[...], wo_ref[...])
    y_ref[0] = _ln(DEEPNORM_ALPHA * x + mix, lng_ref[...], lnb_ref[...])


def _ab_prompt(x, w_in, w_pool, s_pool, gn_g, w_o, ln_g, ln_b):
    b, s, d = x.shape
    tm = min(512, s)
    cos, sin = _rope_tables(np.arange(s))
    tabs = _ret_tables(RET_CHUNK)
    nin = w_in.shape[1]
    in_specs = [
        pl.BlockSpec((1, tm, d), lambda bi, j: (bi, j, 0)),
        _full((d, nin)),
        pl.BlockSpec((tm, 128), lambda bi, j: (j, 0)),
        pl.BlockSpec((tm, 128), lambda bi, j: (j, 0)),
    ] + [_full(t.shape) for t in tabs] + [
        _full(w_pool.shape), _full(s_pool.shape), _full(gn_g.shape), _full(w_o.shape),
        _full(ln_g.shape), _full(ln_b.shape),
    ]
    out_shape = (jax.ShapeDtypeStruct((b, s, d), F32),
                 jax.ShapeDtypeStruct((b, POOL_HIST, POOL_WIDTH), F32),
                 jax.ShapeDtypeStruct((b, RET_QK, RET_V), F32))
    out_specs = (pl.BlockSpec((1, tm, d), lambda bi, j: (bi, j, 0)),
                 pl.BlockSpec((1, POOL_HIST, POOL_WIDTH), lambda bi, j: (bi, 0, 0)),
                 pl.BlockSpec((1, RET_QK, RET_V), lambda bi, j: (bi, 0, 0)))
    return pl.pallas_call(
        functools.partial(_ab_prompt_kernel, tm),
        out_shape=out_shape, grid=(b, s // tm), in_specs=in_specs, out_specs=out_specs,
        scratch_shapes=[pltpu.VMEM((POOL_HIST + tm, POOL_WIDTH), F32),
                        pltpu.VMEM((RET_QK, RET_V), F32),
                        pltpu.VMEM((tm, d), BF16)],
        compiler_params=_cparams("arbitrary", "arbitrary"), name="ab_prompt",
    )(x, w_in, jnp.asarray(cos), jnp.asarray(sin), *tabs, w_pool, s_pool, gn_g, w_o, ln_g, ln_b)


def _ab_sample_kernel(x_ref, win_ref, cos_ref, sin_ref, hmask_ref, gdec_ref, pool_ref, ret_ref,
                      wpool_ref, spool_ref, gn_ref, wo_ref, lng_ref, lnb_ref,
                      y_ref, npool_ref, nret_ref):
    x = x_ref[0]
    h = _dot(x.astype(BF16), win_ref[...])
    u = h[:, :POOL_WIDTH]
    prev = pool_ref[0]
    npool_ref[0, 0:POOL_STATE - 1, :] = prev[1:POOL_STATE, :]
    npool_ref[0, POOL_STATE - 1:POOL_STATE, :] = u[0:1, :]
    mixed = []
    for g, w in enumerate(POOL_WINDOWS):
        lanes = slice(g * POOL_GROUP_DIM, (g + 1) * POOL_GROUP_DIM)
        win = u[0:1, lanes] + jnp.sum(prev[POOL_STATE - (w - 1):, lanes], axis=0, keepdims=True)
        diff = win / float(w) - u[0:1, lanes]
        diff8 = jnp.broadcast_to(diff, (8, POOL_GROUP_DIM))
        mixed.append(_dot(diff8.astype(BF16), wpool_ref[g]) * spool_ref[:, lanes])
    a_out = jnp.concatenate(mixed, axis=1)

    cos = cos_ref[...]
    sin = sin_ref[...]
    qr = _rotary_cat(h[:, POOL_WIDTH:POOL_WIDTH + RET_QK], cos, sin)[0:1]
    kr = (_rotary_cat(h[:, POOL_WIDTH + RET_QK:POOL_WIDTH + 2 * RET_QK], cos, sin) * (RET_QK_DIM ** -0.5))[0:1]
    v = h[0:1, POOL_WIDTH + 2 * RET_QK:POOL_WIDTH + 2 * RET_QK + RET_V]
    gate = h[0:1, POOL_WIDTH + 2 * RET_QK + RET_V:]
    hm = hmask_ref[...]
    gdec = gdec_ref[...]
    sp = ret_ref[0]
    q4 = qr * hm
    k4 = kr * hm
    sc = jnp.sum(q4 * kr, axis=-1, keepdims=True)
    v4 = jnp.concatenate([v[:, hh * RET_V_DIM:(hh + 1) * RET_V_DIM] for hh in range(RET_HEADS)]
                         + [jnp.zeros((8 - RET_HEADS, RET_V_DIM), F32)], axis=0)
    o4 = sc.astype(BF16).astype(F32) * v4.astype(BF16).astype(F32) \
        + _dot((q4 * gdec).astype(BF16), sp.astype(BF16))
    kv = lax.dot_general(k4.astype(BF16), v4.astype(BF16), (((0,), (0,)), ((), ())), preferred_element_type=F32)
    gcol = jnp.sum(hm * gdec, axis=0, keepdims=True)
    nret_ref[0] = sp * gcol.T + kv
    mu = jnp.mean(o4, axis=-1, keepdims=True)
    oc = o4 - mu
    var = jnp.mean(oc * oc, axis=-1, keepdims=True)
    on = oc * lax.rsqrt(var + LN_EPS)
    on_row = jnp.concatenate([on[hh:hh + 1, :] for hh in range(RET_HEADS)], axis=1) * gn_ref[...]
    b_out = jnp.broadcast_to(_silu(gate) * on_row, (8, RET_V))
    cat = jnp.concatenate([a_out, b_out], axis=1).astype(BF16)
    mix = _dot(cat, wo_ref[...])
    y_ref[0] = _ln(DEEPNORM_ALPHA * x + mix, lng_ref[...], lnb_ref[...])


def _ab_sample(x8, w_in, pool_prev, ret_prev_perm, w_pool, s_pool, gn_g, w_o, ln_g, ln_b):
    b, _, d = x8.shape
    cos, sin = _rope_tables(np.array([PAST_LEN]))
    hmask = np.zeros((8, RET_QK), np.float32)
    hmask[:RET_HEADS] = (_lane_head()[None, :] == np.arange(RET_HEADS)[:, None])
    gdec = np.exp(_log_gamma())[_lane_head()][None, :].astype(np.float32)
    per_b = lambda shape: pl.BlockSpec((1,) + shape, lambda i: (i, 0, 0))
    in_specs = [per_b((8, d)), _full(w_in.shape), _full((1, 128)), _full((1, 128)), _full((8, RET_QK)),
                _full((1, RET_QK)), per_b((POOL_STATE, POOL_WIDTH)), per_b((RET_QK, RET_V_DIM)),
                _full(w_pool.shape), _full(s_pool.shape), _full(gn_g.shape), _full(w_o.shape),
                _full(ln_g.shape), _full(ln_b.shape)]
    out_shape = (jax.ShapeDtypeStruct((b, 8, d), F32),
                 jax.ShapeDtypeStruct((b, POOL_STATE, POOL_WIDTH), F32),
                 jax.ShapeDtypeStruct((b, RET_QK, RET_V_DIM), F32))
    out_specs = (per_b((8, d)), per_b((POOL_STATE, POOL_WIDTH)), per_b((RET_QK, RET_V_DIM)))
    return pl.pallas_call(
        _ab_sample_kernel, out_shape=out_shape, grid=(b,), in_specs=in_specs, out_specs=out_specs,
        compiler_params=_cparams("arbitrary"), name="ab_sample",
    )(x8, w_in, jnp.asarray(cos), jnp.asarray(sin), jnp.asarray(hmask), jnp.asarray(gdec),
      pool_prev, ret_prev_perm, w_pool, s_pool, gn_g, w_o, ln_g, ln_b)


def _mem_kv_kernel(m_ref, w_ref, k_ref, v_ref):
    h = _dot(m_ref[...].astype(BF16), w_ref[0])
    k_ref[0] = h[:, :D_MODEL]
    v_ref[0] = h[:, D_MODEL:]


def _mem_kv(mem2d, w_kv):
    n, d = mem2d.shape
    depth = w_kv.shape[0]
    out = jax.ShapeDtypeStruct((depth, n, d), F32)
    return pl.pallas_call(
        _mem_kv_kernel, out_shape=(out, out), grid=(depth,),
        in_specs=[_full((n, d)), pl.BlockSpec((1, d, 2 * d), lambda l: (l, 0, 0))],
        out_specs=(pl.BlockSpec((1, n, d), lambda l: (l, 0, 0)), pl.BlockSpec((1, n, d), lambda l: (l, 0, 0))),
        compiler_params=_cparams("arbitrary"), name="mem_kv",
    )(mem2d, w_kv)


def _mem_attn_kernel(x_ref, wq_ref, k_ref, v_ref, wo_ref, lng_ref, lnb_ref, y_ref):
    x = x_ref[0]
    q = _dot(x.astype(BF16), wq_ref[...]) * (MEM_HEAD_DIM ** -0.5)
    parts = []
    for h in range(MEM_HEADS):
        lanes = slice(h * MEM_HEAD_DIM, (h + 1) * MEM_HEAD_DIM)
        logits = _dot_nt(q[:, lanes].astype(BF16), k_ref[0, :, lanes].astype(BF16))
        m = jnp.max(logits, axis=-1, keepdims=True)
        p = jnp.exp(logits - m)
        l = jnp.sum(p, axis=-1, keepdims=True)
        parts.append(_dot(p.astype(BF16), v_ref[0, :, lanes].astype(BF16)) / l)
    o = jnp.concatenate(parts, axis=1)
    mix = _dot(o.astype(BF16), wo_ref[...])
    y_ref[0] = _ln(DEEPNORM_ALPHA * x + mix, lng_ref[...], lnb_ref[...])


def _mem_attn(x, w_q, k, v, w_o, ln_g, ln_b):
    b, s, d = x.shape
    tm = min(512, s)
    xs = pl.BlockSpec((1, tm, d), lambda bi, j: (bi, j, 0))
    kvs = pl.BlockSpec((1, N_MEM, d), lambda bi, j: (bi, 0, 0))
    return pl.pallas_call(
        _mem_attn_kernel, out_shape=jax.ShapeDtypeStruct((b, s, d), F32), grid=(b, s // tm),
        in_specs=[xs, _full(w_q.shape), kvs, kvs, _full(w_o.shape), _full(ln_g.shape), _full(ln_b.shape)],
        out_specs=xs, compiler_params=_cparams("arbitrary", "arbitrary"), name="mem_attn",
    )(x, w_q, k, v, w_o, ln_g, ln_b)


def _split_bf16(x):
    hi = x.astype(BF16)
    lo = (x - hi.astype(F32)).astype(BF16)
    return hi, lo


def _router_gates(x, wr_hi, wr_lo, br):
    xh, xl = _split_bf16(x)
    lg = _dot(xh, wr_hi) + (_dot(xh, wr_lo) + _dot(xl, wr_hi)) + br
    n = x.shape[0]
    lane_i = lax.broadcasted_iota(jnp.int32, (n, ROUTER_LANES), 1)
    lane = lane_i.astype(F32)
    is_c = lane_i < N_GROUPS
    neg = jnp.float32(-jnp.inf)
    big = jnp.float32(ROUTER_LANES)
    coarse = jnp.where(is_c, lg, neg)
    cmax = jnp.max(coarse, axis=-1, keepdims=True)
    grp = jnp.min(jnp.where(coarse == cmax, lane, big), axis=-1, keepdims=True)
    p_grp = 1.0 / jnp.sum(jnp.where(is_c, jnp.exp(lg - cmax), 0.0), axis=-1, keepdims=True)
    lane_grp = ((lane_i - N_GROUPS) // EXPERTS_PER_GROUP).astype(F32)
    in_grp = (lane_i >= N_GROUPS) & (lane_i < N_GROUPS + N_EXPERTS) & (lane_grp == grp)
    f1 = jnp.where(in_grp, lg, neg)
    t1 = jnp.max(f1, axis=-1, keepdims=True)
    i1 = jnp.min(jnp.where(f1 == t1, lane, big), axis=-1, keepdims=True)
    f2 = jnp.where(lane == i1, neg, f1)
    t2 = jnp.max(f2, axis=-1, keepdims=True)
    i2 = jnp.min(jnp.where(f2 == t2, lane, big), axis=-1, keepdims=True)
    e2 = jnp.exp(t2 - t1)
    w1 = p_grp / (1.0 + e2)
    w2 = p_grp * e2 / (1.0 + e2)
    return jnp.where(lane == i1, w1, 0.0) + jnp.where(lane == i2, w2, 0.0)


def _moe_dense_kernel(x_ref, wrh_ref, wrl_ref, br_ref, wg_ref, wu_ref, wd_ref, lng_ref, lnb_ref, y_ref,
                      xb, gates, acc):
    e = pl.program_id(1)

    @pl.when(e == 0)
    def _():
        x = x_ref[...]
        xb[...] = x.astype(BF16)
        gates[...] = _router_gates(x, wrh_ref[...], wrl_ref[...], br_ref[...])
        acc[...] = jnp.zeros(acc.shape, F32)

    lane_i = lax.broadcasted_iota(jnp.int32, gates.shape, 1)
    gcol = jnp.sum(jnp.where(lane_i == e + N_GROUPS, gates[...], 0.0), axis=-1, keepdims=True)
    xv = xb[...]
    hid = _silu(_dot(xv, wg_ref[0])) * _dot(xv, wu_ref[0]) * gcol
    acc[...] += _dot(hid.astype(BF16), wd_ref[0])

    @pl.when(e == N_EXPERTS - 1)
    def _():
        y_ref[...] = _ln(DEEPNORM_ALPHA * x_ref[...] + acc[...], lng_ref[...], lnb_ref[...])


def _moe_dense(x2d, wr_hi, wr_lo, br, w_gate, w_up, w_down, ln_g, ln_b):
    n, d = x2d.shape
    tm = min(1024, n)
    xs = pl.BlockSpec((tm, d), lambda i, e: (i, 0))
    return pl.pallas_call(
        _moe_dense_kernel, out_shape=jax.ShapeDtypeStruct((n, d), F32), grid=(n // tm, N_EXPERTS),
        in_specs=[xs, _full(wr_hi.shape), _full(wr_lo.shape), _full(br.shape),
                  pl.BlockSpec((1, d, EXPERT_HIDDEN), lambda i, e: (e, 0, 0)),
                  pl.BlockSpec((1, d, EXPERT_HIDDEN), lambda i, e: (e, 0, 0)),
                  pl.BlockSpec((1, EXPERT_HIDDEN, d), lambda i, e: (e, 0, 0)),
                  _full(ln_g.shape), _full(ln_b.shape)],
        out_specs=xs,
        scratch_shapes=[pltpu.VMEM((tm, d), BF16), pltpu.VMEM((tm, ROUTER_LANES), F32), pltpu.VMEM((tm, d), F32)],
        compiler_params=_cparams("arbitrary", "arbitrary"), name="moe_dense",
    )(x2d, wr_hi, wr_lo, br, w_gate, w_up, w_down, ln_g, ln_b)


def _log_sigmoid(z):
    return jnp.minimum(z, 0.0) - jnp.log1p(jnp.exp(-jnp.abs(z)))


def _fox_proj_kernel(x_ref, w_ref, wf_ref, bf_ref, tri_ref, q_ref, k_ref, v_ref, kb_ref, vb_ref, lf_ref, c_ref, carry):
    j = pl.program_id(1)

    @pl.when(j == 0)
    def _():
        carry[...] = jnp.zeros(carry.shape, F32)

    xb = x_ref[0].astype(BF16)
    h = _dot(xb, w_ref[...])
    q_ref[0] = (h[:, :D_MODEL] * (FOX_HEAD_DIM ** -0.5)).astype(BF16)
    k = h[:, D_MODEL:2 * D_MODEL]
    v = h[:, 2 * D_MODEL:]
    k_ref[0] = k
    v_ref[0] = v
    kb_ref[0] = k.astype(BF16)
    vb_ref[0] = v.astype(BF16)
    lf = _log_sigmoid(_dot(xb, wf_ref[...]) + bf_ref[...])
    lf_ref[0] = lf[:, :FOX_HEADS]
    hi = lf.astype(BF16)
    r1 = lf - hi.astype(F32)
    mid = r1.astype(BF16)
    lo = (r1 - mid.astype(F32)).astype(BF16)
    tri = tri_ref[...]
    c = _dot(tri, hi) + (_dot(tri, mid) + _dot(tri, lo)) + carry[...]
    c_ref[0] = c
    carry[...] = c[c.shape[0] - 1:, :]


def _fox_proj(x, w_qkv, w_f, b_f):
    b, s, d = x.shape
    tm = min(512, s)
    tri = jnp.asarray(np.tril(np.ones((tm, tm), np.float32)), dtype=BF16)
    xs = pl.BlockSpec((1, tm, d), lambda bi, j: (bi, j, 0))
    f32o = jax.ShapeDtypeStruct((b, s, d), F32)
    bf16o = jax.ShapeDtypeStruct((b, s, d), BF16)
    return pl.pallas_call(
        _fox_proj_kernel,
        out_shape=(bf16o, f32o, f32o, bf16o, bf16o,
                   jax.ShapeDtypeStruct((b, s, FOX_HEADS), F32), jax.ShapeDtypeStruct((b, s, 128), F32)),
        grid=(b, s // tm),
        in_specs=[xs, _full(w_qkv.shape), _full(w_f.shape), _full(b_f.shape), _full(tri.shape)],
        out_specs=(xs, xs, xs, xs, xs,
                   pl.BlockSpec((1, tm, FOX_HEADS), lambda bi, j: (bi, j, 0)),
                   pl.BlockSpec((1, tm, 128), lambda bi, j: (bi, j, 0))),
        scratch_shapes=[pltpu.VMEM((1, 128), F32)],
        compiler_params=_cparams("arbitrary", "arbitrary"), name="fox_proj",
    )(x, w_qkv, w_f, b_f, tri)


def _fox_attn_kernel(tq, q_ref, k_ref, v_ref, c_ref, ct_ref, o_ref):
    hp = pl.program_id(1)
    i = pl.program_id(2)
    q = q_ref[0]
    lane_q = lax.broadcasted_iota(jnp.int32, (tq, 128), 1)
    first = lane_q < FOX_HEAD_DIM
    zero = jnp.zeros_like(q)
    qh = (jnp.where(first, q, zero), jnp.where(first, zero, q))
    ctile = c_ref[0]
    cq = tuple(jnp.sum(jnp.where(lane_q == 2 * hp + hd, ctile, 0.0), axis=-1, keepdims=True) for hd in range(2))
    qpos = i * tq + lax.broadcasted_iota(jnp.int32, (tq, tq), 0)
    kiota = lax.broadcasted_iota(jnp.int32, (tq, tq), 1)

    def body(jj, carry):
        k0 = pl.multiple_of(jj * tq, tq)
        kt = k_ref[0, pl.ds(k0, tq), :]
        vt = v_ref[0, pl.ds(k0, tq), :]
        causal = (k0 + kiota) <= qpos
        out = []
        for hd in range(2):
            m, l, acc = carry[hd]
            ck = ct_ref[0, 2 * hp + hd, pl.ds(jj, 1), :]
            logits = _dot_nt(qh[hd], kt) + (cq[hd] - ck)
            logits = jnp.where(causal, logits, -jnp.inf)
            m_new = jnp.maximum(m, jnp.max(logits, axis=-1, keepdims=True))
            alpha = jnp.exp(m - m_new)
            p = jnp.exp(logits - m_new)
            l = alpha * l + jnp.sum(p, axis=-1, keepdims=True)
            acc = alpha * acc + _dot(p.astype(BF16), vt)
            out.append((m_new, l, acc))
        return tuple(out)

    init = tuple((jnp.full((tq, 1), -jnp.inf, F32), jnp.zeros((tq, 1), F32), jnp.zeros((tq, 128), F32))
                 for _ in range(2))
    res = lax.fori_loop(0, i + 1, body, init)
    o0 = res[0][2] / res[0][1]
    o1 = res[1][2] / res[1][1]
    o_ref[0] = jnp.where(first, o0, o1).astype(BF16)


def _fox_attn(q, kb, vb, c, ct):
    b, s, d = q.shape
    tq = min(256, s)
    nq = s // tq
    qs = pl.BlockSpec((1, tq, 128), lambda bi, hp, i: (bi, i, hp))
    kvs = pl.BlockSpec((1, s, 128), lambda bi, hp, i: (bi, 0, hp))
    return pl.pallas_call(
        functools.partial(_fox_attn_kernel, tq),
        out_shape=jax.ShapeDtypeStruct((b, s, d), BF16), grid=(b, FOX_HEADS // 2, nq),
        in_specs=[qs, kvs, kvs,
                  pl.BlockSpec((1, tq, 128), lambda bi, hp, i: (bi, i, 0)),
                  pl.BlockSpec((1, FOX_HEADS, nq, tq), lambda bi, hp, i: (bi, 0, 0, 0))],
        out_specs=qs, compiler_params=_cparams("arbitrary", "arbitrary", "arbitrary"), name="fox_attn",
    )(q, kb, vb, c, ct)


def _proj_ln_kernel(o_ref, x_ref, w_ref, lng_ref, lnb_ref, y_ref):
    mix = _dot(o_ref[...].astype(BF16), w_ref[...])
    y_ref[...] = _ln(DEEPNORM_ALPHA * x_ref[...] + mix, lng_ref[...], lnb_ref[...])


def _proj_ln(o2d, x2d, w, ln_g, ln_b):
    n, d = x2d.shape
    tm = min(1024, n)
    xs = pl.BlockSpec((tm, d), lambda i: (i, 0))
    return pl.pallas_call(
        _proj_ln_kernel, out_shape=jax.ShapeDtypeStruct((n, d), F32), grid=(n // tm,),
        in_specs=[xs, xs, _full(w.shape), _full(ln_g.shape), _full(ln_b.shape)], out_specs=xs,
        compiler_params=_cparams("arbitrary"), name="proj_ln",
    )(o2d, x2d, w, ln_g, ln_b)


def _fox_proj_sample_kernel(x_ref, w_ref, wf_ref, bf_ref, q_ref, k_ref, v_ref, lf_ref):
    xb = x_ref[...].astype(BF16)
    h = _dot(xb, w_ref[...])
    q_ref[...] = h[:, :D_MODEL] * (FOX_HEAD_DIM ** -0.5)
    k_ref[...] = h[:, D_MODEL:2 * D_MODEL]
    v_ref[...] = h[:, 2 * D_MODEL:]
    lf_ref[...] = _log_sigmoid(_dot(xb, wf_ref[...]) + bf_ref[...])


def _fox_proj_sample(x2d, w_qkv, w_f, b_f):
    n, d = x2d.shape
    o = jax.ShapeDtypeStruct((n, d), F32)
    return pl.pallas_call(
        _fox_proj_sample_kernel, out_shape=(o, o, o, jax.ShapeDtypeStruct((n, 128), F32)),
        name="fox_proj_sample", compiler_params=pltpu.CompilerParams(vmem_limit_bytes=VMEM_LIMIT),
    )(x2d, w_qkv, w_f, b_f)


PAGES_PER_STEP = 8
SCAN_PAD = 64


def _split3(x):
    hi = x.astype(BF16)
    r1 = x - hi.astype(F32)
    mid = r1.astype(BF16)
    lo = (r1 - mid.astype(F32)).astype(BF16)
    return hi, mid, lo


def _expand_heads(row, ex):
    hi, mid, lo = _split3(jnp.broadcast_to(row, (8, 128)))
    return (_dot(hi, ex) + (_dot(mid, ex) + _dot(lo, ex)))[0:1, :]


def _fox_paged_kernel(npp, pt_ref, q_ref, kn_ref, vn_ref, lfn_ref, hm_ref, ex_ref, *rest):
    del pt_ref
    k_refs = rest[:npp]
    v_refs = rest[npp:2 * npp]
    lf_refs = rest[2 * npp:3 * npp]
    o_ref = rest[3 * npp]
    qblk, m_s, l_s, acc_s, ccar, scan = rest[3 * npp + 1:]
    s = pl.program_id(1)
    ns = pl.num_programs(1)
    ex = ex_ref[...]

    @pl.when(s == 0)
    def _():
        qrows = jnp.broadcast_to(q_ref[0], (128, D_MODEL)) * hm_ref[...]
        qblk[...] = qrows.T.astype(BF16)
        m_s[...] = jnp.full(m_s.shape, -jnp.inf, F32)
        l_s[...] = jnp.zeros(l_s.shape, F32)
        acc_s[...] = jnp.zeros(acc_s.shape, F32)
        ccar[...] = jnp.zeros(ccar.shape, F32)
        scan[...] = jnp.zeros(scan.shape, F32)

    for p in range(npp):
        scan[p, SCAN_PAD:, 0:FOX_HEADS] = lf_refs[p][0]
    cs = scan[:, SCAN_PAD:, :]
    sh = 1
    while sh < PAGE_SIZE:
        if sh > 1:
            scan[:, SCAN_PAD:, :] = cs
        cs = cs + scan[:, SCAN_PAD - sh:SCAN_PAD - sh + PAGE_SIZE, :]
        sh *= 2
    base = ccar[...]
    cpages = []
    for p in range(npp):
        cpages.append(cs[p] + base)
        base = base + cs[p, PAGE_SIZE - 1:PAGE_SIZE, :]
    ccar[...] = base
    c = jnp.concatenate(cpages, axis=0)

    kcat = jnp.concatenate([r[0].astype(BF16) for r in k_refs], axis=0)
    logits = _dot(kcat, qblk[...]) - c
    m_old = m_s[...]
    m_new = jnp.maximum(m_old, jnp.max(logits, axis=0, keepdims=True))
    alpha = jnp.exp(m_old - m_new)
    p = jnp.exp(logits - m_new)
    l_s[...] = alpha * l_s[...] + jnp.sum(p, axis=0, keepdims=True)
    m_s[...] = m_new
    pe = _dot(p.astype(BF16), ex)
    vcat = jnp.concatenate([r[0] for r in v_refs], axis=0)
    contrib = jnp.sum((pe * vcat).reshape(npp * PAGE_SIZE // 8, 8, D_MODEL), axis=0)
    acc_s[...] = acc_s[...] * _expand_heads(alpha, ex) + contrib

    @pl.when(s == ns - 1)
    def _():
        c_new = ccar[...] + lfn_ref[0]
        kn = jnp.broadcast_to(kn_ref[0], (8, D_MODEL)).astype(BF16)
        ln = _dot(kn, qblk[...])[0:1, :] - c_new
        m_prev = m_s[...]
        m_fin = jnp.maximum(m_prev, ln)
        a2 = jnp.exp(m_prev - m_fin)
        pn = jnp.exp(ln - m_fin)
        l_fin = a2 * l_s[...] + pn
        pne = _dot(jnp.broadcast_to(pn, (8, 128)).astype(BF16), ex)[0:1, :]
        tot = jnp.sum(acc_s[...], axis=0, keepdims=True) * _expand_heads(a2, ex) + pne * vn_ref[0]
        o_ref[0] = tot / _expand_heads(l_fin, ex)


def _fox_paged(q, k_new, v_new, lf_new, cache_k, cache_v, cache_lf, page_table):
    b, d = q.shape
    n_pages = page_table.shape[1]
    npp = PAGES_PER_STEP
    hm = np.zeros((128, d), np.float32)
    for h in range(FOX_HEADS):
        hm[h, h * FOX_HEAD_DIM:(h + 1) * FOX_HEAD_DIM] = 1.0
    row = lambda w: pl.BlockSpec((1, 1, w), lambda bi, s, pt: (bi, 0, 0))
    const = lambda shape: pl.BlockSpec(shape, lambda bi, s, pt: (0,) * len(shape))

    def page_spec(p, w):
        return pl.BlockSpec((1, PAGE_SIZE, w), lambda bi, s, pt: (pt[bi * n_pages + s * npp + p], 0, 0))

    in_specs = ([row(d), row(d), row(d), row(128), const((128, d)), const((128, d))]
                + [page_spec(p, d) for p in range(npp)] + [page_spec(p, d) for p in range(npp)]
                + [page_spec(p, FOX_HEADS) for p in range(npp)])
    grid_spec = pltpu.PrefetchScalarGridSpec(
        num_scalar_prefetch=1, grid=(b, n_pages // npp), in_specs=in_specs, out_specs=row(d),
        scratch_shapes=[pltpu.VMEM((d, 128), BF16), pltpu.VMEM((1, 128), F32), pltpu.VMEM((1, 128), F32),
                        pltpu.VMEM((8, d), F32), pltpu.VMEM((1, 128), F32),
                        pltpu.VMEM((npp, SCAN_PAD + PAGE_SIZE, 128), F32)])
    out = pl.pallas_call(
        functools.partial(_fox_paged_kernel, npp), out_shape=jax.ShapeDtypeStruct((b, 1, d), F32),
        grid_spec=grid_spec, compiler_params=_cparams("arbitrary", "arbitrary"), name="fox_paged",
    )(page_table.reshape(-1), q[:, None, :], k_new[:, None, :], v_new[:, None, :], lf_new[:, None, :],
      jnp.asarray(hm), jnp.asarray(hm, dtype=BF16),
      *([cache_k] * npp), *([cache_v] * npp), *([cache_lf] * npp))
    return out[:, 0, :]


def _router_weights(w_gc, b_gc, w_gf, b_gf):
    d = w_gc.shape[0]
    wf = jnp.transpose(w_gf, (1, 0, 2)).reshape(d, N_EXPERTS)
    wr = jnp.zeros((d, ROUTER_LANES), F32).at[:, :N_GROUPS].set(w_gc).at[:, N_GROUPS:N_GROUPS + N_EXPERTS].set(wf)
    br = jnp.zeros((1, ROUTER_LANES), F32).at[0, :N_GROUPS].set(b_gc).at[0, N_GROUPS:N_GROUPS + N_EXPERTS].set(
        b_gf.reshape(-1))
    hi = wr.astype(BF16)
    lo = (wr - hi.astype(F32)).astype(BF16)
    return hi, lo, br


def kernel(x_prompt, x_sample, mem_prompt, state_pool, state_ret, cache_fox_k, cache_fox_v, cache_fox_logf,
           cache_mem_k, cache_mem_v, page_table, ln_g, ln_b, ab_w_in, pool_w, pool_scale, ret_gn_g, ab_w_o,
           fox_w_in, fox_b_f, fox_w_o, mem_wq, mem_wkv, mem_wo, moe_w_gc, moe_b_gc, moe_w_gf, moe_b_gf,
           moe_w_up, moe_w_gate, moe_w_down):
    bp, s, d = x_prompt.shape
    bs = x_sample.shape[0]
    n_phys = cache_fox_k.shape[1]
    row = lambda a: a.reshape(1, -1)
    rows8 = lambda y2: jnp.broadcast_to(y2[:, None, :], (bs, 8, d))

    yp = x_prompt
    ys = x_sample.reshape(bs, d)
    mk_all, mv_all = _mem_kv(mem_prompt.reshape(bp * N_MEM, d), mem_wkv.astype(BF16))

    perm = _qk_perm()
    inv_perm = np.argsort(perm)
    for l in range(DEPTH):
        g0, b0 = row(ln_g[l, 0]), row(ln_b[l, 0])
        if l % 2 == 0:
            e = l // 2
            cols = np.concatenate([np.arange(POOL_WIDTH), POOL_WIDTH + perm, POOL_WIDTH + RET_QK + perm,
                                   np.arange(POOL_WIDTH + 2 * RET_QK, ab_w_in.shape[-1])])
            w_in = ab_w_in[e][:, cols].astype(BF16)
            w_pool = pool_w[e].astype(BF16)
            w_o = ab_w_o[e].astype(BF16)
            yp, pool_p, ret_p = _ab_prompt(yp, w_in, w_pool, row(pool_scale[e]), row(ret_gn_g[e]), w_o, g0, b0)
            ret_prev = state_ret[e].reshape(bs, RET_QK, RET_V_DIM)[:, perm, :]
            y8, pool_s, ret_s = _ab_sample(rows8(ys), w_in, state_pool[e], ret_prev, w_pool, row(pool_scale[e]),
                                           row(ret_gn_g[e]), w_o, g0, b0)
            ys = y8[:, 0, :]
            new_pool_p = pool_p[:, POOL_HIST - POOL_STATE:, :][None]
            new_pool_s = pool_s[None]
            rp = ret_p[:, inv_perm, :].reshape(bp, RET_HEADS, RET_QK_DIM, RET_HEADS, RET_V_DIM)
            new_ret_p = jnp.stack([rp[:, h, :, h, :] for h in range(RET_HEADS)], axis=1)[None]
            new_ret_s = ret_s[:, inv_perm, :].reshape(bs, RET_HEADS, RET_QK_DIM, RET_V_DIM)[None]
        else:
            o = l // 2
            w = fox_w_in[o]
            w_qkv = w[:, :3 * d].astype(BF16)
            w_f = jnp.pad(w[:, 3 * d:], ((0, 0), (0, 128 - FOX_HEADS))).astype(BF16)
            b_f = jnp.pad(fox_b_f[o], (0, 128 - FOX_HEADS)).reshape(1, 128)
            w_o = fox_w_o[o].astype(BF16)
            q, k, v, kb, vb, lf, c = _fox_proj(yp, w_qkv, w_f, b_f)
            tq = min(256, s)
            ct = jnp.transpose(c[:, :, :FOX_HEADS], (0, 2, 1)).reshape(bp, FOX_HEADS, s // tq, tq)
            att = _fox_attn(q, kb, vb, c, ct)
            yp = _proj_ln(att.reshape(bp * s, d), yp.reshape(bp * s, d), w_o, g0, b0).reshape(bp, s, d)
            new_fk_p = k.reshape(1, bp, s, FOX_HEADS, FOX_HEAD_DIM)
            new_fv_p = v.reshape(1, bp, s, FOX_HEADS, FOX_HEAD_DIM)
            new_fl_p = lf[None]
            qs, ks, vs, lfs = _fox_proj_sample(ys, w_qkv, w_f, b_f)
            att_s = _fox_paged(qs, ks, vs, lfs, cache_fox_k[o].reshape(n_phys, PAGE_SIZE, d),
                               cache_fox_v[o].reshape(n_phys, PAGE_SIZE, d), cache_fox_logf[o], page_table)
            ys = _proj_ln(att_s, ys, w_o, g0, b0)
            new_fk_s = ks.reshape(1, bs, 1, FOX_HEADS, FOX_HEAD_DIM)
            new_fv_s = vs.reshape(1, bs, 1, FOX_HEADS, FOX_HEAD_DIM)
            new_fl_s = lfs[:, :FOX_HEADS].reshape(1, bs, 1, FOX_HEADS)

        g1, b1 = row(ln_g[l, 1]), row(ln_b[l, 1])
        wq = mem_wq[l].astype(BF16)
        wo = mem_wo[l].astype(BF16)
        yp = _mem_attn(yp, wq, mk_all[l].reshape(bp, N_MEM, d), mv_all[l].reshape(bp, N_MEM, d), wo, g1, b1)
        ys = _mem_attn(rows8(ys), wq, cache_mem_k[l].reshape(bs, N_MEM, d), cache_mem_v[l].reshape(bs, N_MEM, d),
                       wo, g1, b1)[:, 0, :]

        g2, b2 = row(ln_g[l, 2]), row(ln_b[l, 2])
        wr_hi, wr_lo, br = _router_weights(moe_w_gc[l], moe_b_gc[l], moe_w_gf[l], moe_b_gf[l])
        wg, wu, wd = moe_w_gate[l].astype(BF16), moe_w_up[l].astype(BF16), moe_w_down[l].astype(BF16)
        yp = _moe_dense(yp.reshape(bp * s, d), wr_hi, wr_lo, br, wg, wu, wd, g2, b2).reshape(bp, s, d)
        ys = _moe_dense(ys, wr_hi, wr_lo, br, wg, wu, wd, g2, b2)

    new_mk_p = mk_all.reshape(DEPTH, bp, N_MEM, MEM_HEADS, MEM_HEAD_DIM)
    new_mv_p = mv_all.reshape(DEPTH, bp, N_MEM, MEM_HEADS, MEM_HEAD_DIM)
    return (yp, ys.reshape(bs, 1, d), new_pool_p, new_pool_s, new_ret_p, new_ret_s, new_fk_p, new_fk_s,
            new_fv_p, new_fv_s, new_fl_p, new_fl_s, new_mk_p, new_mv_p)
```

```python
import functools

import numpy as np
import jax
import jax.numpy as jnp
from jax import lax
from jax.experimental import pallas as pl
from jax.experimental.pallas import tpu as pltpu

F32 = jnp.float32
BF16 = jnp.bfloat16

D_MODEL = 1024
DEPTH = 2
PAST_LEN = 8192
PAGE_SIZE = 128

POOL_WIDTH = 512
POOL_GROUP_DIM = 128
POOL_WINDOWS = (2, 4, 8, 16)
POOL_STATE = 15
POOL_HIST = 16

RET_HEADS = 4
RET_QK_DIM = 64
RET_V_DIM = 128
RET_QK = 256
RET_V = 512
RET_CHUNK = 128
ROPE_BASE = 10000.0
ROPE_HALF = RET_QK_DIM // 2

FOX_HEADS = 16
FOX_HEAD_DIM = 64

N_MEM = 256
MEM_HEADS = 4
MEM_HEAD_DIM = 256

N_GROUPS = 4
EXPERTS_PER_GROUP = 4
N_EXPERTS = 16
EXPERT_HIDDEN = 256
ROUTER_LANES = 128

DEEPNORM_ALPHA = (2 * DEPTH) ** 0.25
LN_EPS = 1e-5
LOG2E = 1.4426950408889634

VMEM_LIMIT = 56 * 1024 * 1024


def _cparams(*sem):
    return pltpu.CompilerParams(dimension_semantics=sem, vmem_limit_bytes=VMEM_LIMIT)


def _full(shape):
    n = len(shape)
    return pl.BlockSpec(shape, lambda *_: (0,) * n)


def _ln(z, g, b):
    mu = jnp.mean(z, axis=-1, keepdims=True)
    zc = z - mu
    var = jnp.mean(zc * zc, axis=-1, keepdims=True)
    return zc * lax.rsqrt(var + LN_EPS) * g + b


def _dot(a, b):
    return jnp.dot(a, b, preferred_element_type=F32)


def _dot_nt(a, b):
    return lax.dot_general(a, b, (((1,), (1,)), ((), ())), preferred_element_type=F32)


def _silu(x):
    return x * jax.nn.sigmoid(x)


def _split_bf16(x):
    hi = x.astype(BF16)
    lo = (x - hi.astype(F32)).astype(BF16)
    return hi, lo


def _split3(x):
    hi = x.astype(BF16)
    r1 = x - hi.astype(F32)
    mid = r1.astype(BF16)
    lo = (r1 - mid.astype(F32)).astype(BF16)
    return hi, mid, lo


def _qk_perm():
    p = np.arange(RET_QK)
    h = (p % 128) // ROPE_HALF
    d = (p % ROPE_HALF) + ROPE_HALF * (p // 128)
    return h * RET_QK_DIM + d


def _lane_head():
    return (np.arange(RET_QK) % 128) // ROPE_HALF


def _log_gamma():
    return np.log(1.0 - 2.0 ** (-5.0 - np.arange(RET_HEADS, dtype=np.float64)))


def _rope_tables(pos):
    inv = ROPE_BASE ** (-np.arange(ROPE_HALF, dtype=np.float64) / ROPE_HALF)
    ang = pos.astype(np.float64)[:, None] * inv[None, :]
    cos = np.tile(np.cos(ang).astype(np.float32), (1, RET_HEADS))
    sin = np.tile(np.sin(ang).astype(np.float32), (1, RET_HEADS))
    return cos, sin


def _ret_tables(chunk):
    lg = _log_gamma()
    lh = _lane_head()
    i = np.arange(chunk, dtype=np.float64)
    dist = i[:, None] - i[None, :]
    dmask = np.where(dist >= 0, np.exp(lg[:, None, None] * np.maximum(dist, 0.0)), 0.0)
    qdec = np.exp(lg[lh][None, :] * (i[:, None] + 1.0))
    kdec = np.exp(lg[lh][None, :] * (chunk - 1.0 - i[:, None]))
    sdec = np.repeat(np.exp(lg[lh] * chunk)[:, None], RET_V, axis=1)
    bmask = (lh[:, None] == (np.arange(RET_V) // RET_V_DIM)[None, :]).astype(np.float64)
    hmask = (lh[None, :] == np.arange(RET_HEADS)[:, None]).astype(np.float64)
    f = lambda a: jnp.asarray(a.astype(np.float32))
    return f(dmask), f(qdec), f(kdec), f(sdec), f(bmask), f(hmask)


def _rotary_cat(x, cos, sin):
    x1, x2 = x[:, :128], x[:, 128:]
    return jnp.concatenate([x1 * cos - x2 * sin, x1 * sin + x2 * cos], axis=1)


def _head_norm(o, gn):
    parts = []
    for h in range(RET_HEADS):
        oh = o[:, h * RET_V_DIM:(h + 1) * RET_V_DIM]
        mu = jnp.mean(oh, axis=-1, keepdims=True)
        oc = oh - mu
        var = jnp.mean(oc * oc, axis=-1, keepdims=True)
        parts.append(oc * lax.rsqrt(var + LN_EPS))
    return jnp.concatenate(parts, axis=1) * gn


def _ab_prompt_kernel(tm, x_ref, win_ref, cos_ref, sin_ref, dmask_ref, qdec_ref, kdec_ref, sdec_ref, bmask_ref,
                      hmask_ref, wpool_ref, spool_ref, gn_ref, wo_ref, lng_ref, lnb_ref,
                      y_ref, pool_ref, ret_ref, uext, state, cat):
    j = pl.program_id(1)

    @pl.when(j == 0)
    def _():
        uext[0:POOL_HIST, :] = jnp.zeros((POOL_HIST, POOL_WIDTH), F32)
        state[...] = jnp.zeros(state.shape, F32)

    x = x_ref[0]
    h = _dot(x.astype(BF16), win_ref[...])
    u = h[:, :POOL_WIDTH]
    uext[POOL_HIST:POOL_HIST + tm, :] = u

    pos = j * tm + lax.broadcasted_iota(jnp.int32, (tm, 1), 0)
    for g, w in enumerate(POOL_WINDOWS):
        lanes = slice(g * POOL_GROUP_DIM, (g + 1) * POOL_GROUP_DIM)
        win = u[:, lanes]
        for s in range(1, w):
            win = win + uext[POOL_HIST - s:POOL_HIST - s + tm, lanes]
        cnt = jnp.minimum(w, pos + 1).astype(F32)
        diff = win / cnt - u[:, lanes]
        mixed = _dot(diff.astype(BF16), wpool_ref[g]) * spool_ref[:, lanes]
        cat[:, lanes] = mixed.astype(BF16)
    tail = uext[tm:tm + POOL_HIST, :]
    uext[0:POOL_HIST, :] = tail
    pool_ref[0] = tail

    cos = cos_ref[...]
    sin = sin_ref[...]
    qr = _rotary_cat(h[:, POOL_WIDTH:POOL_WIDTH + RET_QK], cos, sin)
    kr = _rotary_cat(h[:, POOL_WIDTH + RET_QK:POOL_WIDTH + 2 * RET_QK], cos, sin) * (RET_QK_DIM ** -0.5)
    v = h[:, POOL_WIDTH + 2 * RET_QK:POOL_WIDTH + 2 * RET_QK + RET_V]
    gate = h[:, POOL_WIDTH + 2 * RET_QK + RET_V:]
    gn = gn_ref[...]
    c = RET_CHUNK
    for ci in range(tm // c):
        rows = slice(ci * c, (ci + 1) * c)
        qc, kc = qr[rows], kr[rows]
        kcb = kc.astype(BF16)
        vcb = v[rows].astype(BF16)
        sm = state[...]
        o = _dot((qc * qdec_ref[...]).astype(BF16), sm.astype(BF16))
        parts = []
        for hh in range(RET_HEADS):
            qm = (qc * hmask_ref[hh:hh + 1, :]).astype(BF16)
            sc = _dot_nt(qm, kcb) * dmask_ref[hh]
            parts.append(_dot(sc.astype(BF16), vcb[:, hh * RET_V_DIM:(hh + 1) * RET_V_DIM]))
        o = o + jnp.concatenate(parts, axis=1)
        kdt = (kc * kdec_ref[...]).T.astype(BF16)
        state[...] = sm * sdec_ref[...] + _dot(kdt, vcb) * bmask_ref[...]
        cat[rows, POOL_WIDTH:] = (_silu(gate[rows]) * _head_norm(o, gn)).astype(BF16)
    ret_ref[0] = state[...]

    mix = _dot(cat[...], wo_ref[...])
    y_ref[0] = _ln(DEEPNORM_ALPHA * x + mix, lng_ref[...], lnb_ref[...])


def _ab_prompt(x, w_in, w_pool, s_pool, gn_g, w_o, ln_g, ln_b):
    b, s, d = x.shape
    tm = min(512, s)
    cos, sin = _rope_tables(np.arange(s))
    tabs = _ret_tables(RET_CHUNK)
    nin = w_in.shape[1]
    in_specs = [
        pl.BlockSpec((1, tm, d), lambda bi, j: (bi, j, 0)),
        _full((d, nin)),
        pl.BlockSpec((tm, 128), lambda bi, j: (j, 0)),
        pl.BlockSpec((tm, 128), lambda bi, j: (j, 0)),
    ] + [_full(t.shape) for t in tabs] + [
        _full(w_pool.shape), _full(s_pool.shape), _full(gn_g.shape), _full(w_o.shape),
        _full(ln_g.shape), _full(ln_b.shape),
    ]
    out_shape = (jax.ShapeDtypeStruct((b, s, d), F32),
                 jax.ShapeDtypeStruct((b, POOL_HIST, POOL_WIDTH), F32),
                 jax.ShapeDtypeStruct((b, RET_QK, RET_V), F32))
    out_specs = (pl.BlockSpec((1, tm, d), lambda bi, j: (bi, j, 0)),
                 pl.BlockSpec((1, POOL_HIST, POOL_WIDTH), lambda bi, j: (bi, 0, 0)),
                 pl.BlockSpec((1, RET_QK, RET_V), lambda bi, j: (bi, 0, 0)))
    return pl.pallas_call(
        functools.partial(_ab_prompt_kernel, tm),
        out_shape=out_shape, grid=(b, s // tm), in_specs=in_specs, out_specs=out_specs,
        scratch_shapes=[pltpu.VMEM((POOL_HIST + tm, POOL_WIDTH), F32),
                        pltpu.VMEM((RET_QK, RET_V), F32),
                        pltpu.VMEM((tm, d), BF16)],
        compiler_params=_cparams("arbitrary", "arbitrary"), name="ab_prompt",
    )(x, w_in, jnp.asarray(cos), jnp.asarray(sin), *tabs, w_pool, s_pool, gn_g, w_o, ln_g, ln_b)


def _ab_sample_kernel(x_ref, win_ref, cos_ref, sin_ref, hmask_ref, gdec_ref, pool_ref, ret_ref,
                      wpool_ref, spool_ref, gn_ref, wo_ref, lng_ref, lnb_ref,
                      y_ref, npool_ref, nret_ref):
    x = x_ref[0]
    h = _dot(x.astype(BF16), win_ref[...])
    u = h[:, :POOL_WIDTH]
    prev = pool_ref[0]
    npool_ref[0, 0:POOL_STATE - 1, :] = prev[1:POOL_STATE, :]
    npool_ref[0, POOL_STATE - 1:POOL_STATE, :] = u[0:1, :]
    mixed = []
    for g, w in enumerate(POOL_WINDOWS):
        lanes = slice(g * POOL_GROUP_DIM, (g + 1) * POOL_GROUP_DIM)
        win = u[0:1, lanes] + jnp.sum(prev[POOL_STATE - (w - 1):, lanes], axis=0, keepdims=True)
        diff = win / float(w) - u[0:1, lanes]
        diff8 = jnp.broadcast_to(diff, (8, POOL_GROUP_DIM))
        mixed.append(_dot(diff8.astype(BF16), wpool_ref[g]) * spool_ref[:, lanes])
    a_out = jnp.concatenate(mixed, axis=1)

    cos = cos_ref[...]
    sin = sin_ref[...]
    qr = _rotary_cat(h[:, POOL_WIDTH:POOL_WIDTH + RET_QK], cos, sin)[0:1]
    kr = (_rotary_cat(h[:, POOL_WIDTH + RET_QK:POOL_WIDTH + 2 * RET_QK], cos, sin) * (RET_QK_DIM ** -0.5))[0:1]
    v = h[0:1, POOL_WIDTH + 2 * RET_QK:POOL_WIDTH + 2 * RET_QK + RET_V]
    gate = h[0:1, POOL_WIDTH + 2 * RET_QK + RET_V:]
    hm = hmask_ref[...]
    gdec = gdec_ref[...]
    sp = ret_ref[0]
    q4 = qr * hm
    k4 = kr * hm
    sc = jnp.sum(q4 * kr, axis=-1, keepdims=True)
    v4 = jnp.concatenate([v[:, hh * RET_V_DIM:(hh + 1) * RET_V_DIM] for hh in range(RET_HEADS)]
                         + [jnp.zeros((8 - RET_HEADS, RET_V_DIM), F32)], axis=0)
    o4 = sc.astype(BF16).astype(F32) * v4.astype(BF16).astype(F32) \
        + _dot((q4 * gdec).astype(BF16), sp.astype(BF16))
    kv = lax.dot_general(k4.astype(BF16), v4.astype(BF16), (((0,), (0,)), ((), ())), preferred_element_type=F32)
    gcol = jnp.sum(hm * gdec, axis=0, keepdims=True)
    nret_ref[0] = sp * gcol.T + kv
    mu = jnp.mean(o4, axis=-1, keepdims=True)
    oc = o4 - mu
    var = jnp.mean(oc * oc, axis=-1, keepdims=True)
    on = oc * lax.rsqrt(var + LN_EPS)
    on_row = jnp.concatenate([on[hh:hh + 1, :] for hh in range(RET_HEADS)], axis=1) * gn_ref[...]
    b_out = jnp.broadcast_to(_silu(gate) * on_row, (8, RET_V))
    cat = jnp.concatenate([a_out, b_out], axis=1).astype(BF16)
    mix = _dot(cat, wo_ref[...])
    y_ref[0] = _ln(DEEPNORM_ALPHA * x + mix, lng_ref[...], lnb_ref[...])


def _ab_sample(x8, w_in, pool_prev, ret_prev_perm, w_pool, s_pool, gn_g, w_o, ln_g, ln_b):
    b, _, d = x8.shape
    cos, sin = _rope_tables(np.array([PAST_LEN]))
    hmask = np.zeros((8, RET_QK), np.float32)
    hmask[:RET_HEADS] = (_lane_head()[None, :] == np.arange(RET_HEADS)[:, None])
    gdec = np.exp(_log_gamma())[_lane_head()][None, :].astype(np.float32)
    per_b = lambda shape: pl.BlockSpec((1,) + shape, lambda i: (i, 0, 0))
    in_specs = [per_b((8, d)), _full(w_in.shape), _full((1, 128)), _full((1, 128)), _full((8, RET_QK)),
                _full((1, RET_QK)), per_b((POOL_STATE, POOL_WIDTH)), per_b((RET_QK, RET_V_DIM)),
                _full(w_pool.shape), _full(s_pool.shape), _full(gn_g.shape), _full(w_o.shape),
                _full(ln_g.shape), _full(ln_b.shape)]
    out_shape = (jax.ShapeDtypeStruct((b, 8, d), F32),
                 jax.ShapeDtypeStruct((b, POOL_STATE, POOL_WIDTH), F32),
                 jax.ShapeDtypeStruct((b, RET_QK, RET_V_DIM), F32))
    out_specs = (per_b((8, d)), per_b((POOL_STATE, POOL_WIDTH)), per_b((RET_QK, RET_V_DIM)))
    return pl.pallas_call(
        _ab_sample_kernel, out_shape=out_shape, grid=(b,), in_specs=in_specs, out_specs=out_specs,
        compiler_params=_cparams("arbitrary"), name="ab_sample",
    )(x8, w_in, jnp.asarray(cos), jnp.asarray(sin), jnp.asarray(hmask), jnp.asarray(gdec),
      pool_prev, ret_prev_perm, w_pool, s_pool, gn_g, w_o, ln_g, ln_b)


def _mem_kv_kernel(m_ref, w_ref, k_ref, v_ref):
    h = _dot(m_ref[...].astype(BF16), w_ref[0])
    k_ref[0] = h[:, :D_MODEL]
    v_ref[0] = h[:, D_MODEL:]


def _mem_kv(mem2d, w_kv):
    n, d = mem2d.shape
    depth = w_kv.shape[0]
    out = jax.ShapeDtypeStruct((depth, n, d), F32)
    return pl.pallas_call(
        _mem_kv_kernel, out_shape=(out, out), grid=(depth,),
        in_specs=[_full((n, d)), pl.BlockSpec((1, d, 2 * d), lambda l: (l, 0, 0))],
        out_specs=(pl.BlockSpec((1, n, d), lambda l: (l, 0, 0)), pl.BlockSpec((1, n, d), lambda l: (l, 0, 0))),
        compiler_params=_cparams("arbitrary"), name="mem_kv",
    )(mem2d, w_kv)


def _mem_attn_kernel(x_ref, wq_ref, k_ref, v_ref, wo_ref, lng_ref, lnb_ref, y_ref):
    x = x_ref[0]
    q = _dot(x.astype(BF16), wq_ref[...]) * (MEM_HEAD_DIM ** -0.5)
    parts = []
    for h in range(MEM_HEADS):
        lanes = slice(h * MEM_HEAD_DIM, (h + 1) * MEM_HEAD_DIM)
        logits = _dot_nt(q[:, lanes].astype(BF16), k_ref[0, :, lanes].astype(BF16))
        m = jnp.max(logits, axis=-1, keepdims=True)
        p = jnp.exp(logits - m)
        l = jnp.sum(p, axis=-1, keepdims=True)
        parts.append(_dot(p.astype(BF16), v_ref[0, :, lanes].astype(BF16)) / l)
    o = jnp.concatenate(parts, axis=1)
    mix = _dot(o.astype(BF16), wo_ref[...])
    y_ref[0] = _ln(DEEPNORM_ALPHA * x + mix, lng_ref[...], lnb_ref[...])


def _mem_attn(x, w_q, k, v, w_o, ln_g, ln_b):
    b, s, d = x.shape
    tm = min(512, s)
    xs = pl.BlockSpec((1, tm, d), lambda bi, j: (bi, j, 0))
    kvs = pl.BlockSpec((1, N_MEM, d), lambda bi, j: (bi, 0, 0))
    return pl.pallas_call(
        _mem_attn_kernel, out_shape=jax.ShapeDtypeStruct((b, s, d), F32), grid=(b, s // tm),
        in_specs=[xs, _full(w_q.shape), kvs, kvs, _full(w_o.shape), _full(ln_g.shape), _full(ln_b.shape)],
        out_specs=xs, compiler_params=_cparams("arbitrary", "arbitrary"), name="mem_attn",
    )(x, w_q, k, v, w_o, ln_g, ln_b)


def _router_gates(x, wr_hi, wr_lo, br):
    xh, xl = _split_bf16(x)
    lg = _dot(xh, wr_hi) + (_dot(xh, wr_lo) + _dot(xl, wr_hi)) + br
    n = x.shape[0]
    lane_i = lax.broadcasted_iota(jnp.int32, (n, ROUTER_LANES), 1)
    lane = lane_i.astype(F32)
    is_c = lane_i < N_GROUPS
    neg = jnp.float32(-jnp.inf)
    big = jnp.float32(ROUTER_LANES)
    coarse = jnp.where(is_c, lg, neg)
    cmax = jnp.max(coarse, axis=-1, keepdims=True)
    grp = jnp.min(jnp.where(coarse == cmax, lane, big), axis=-1, keepdims=True)
    p_grp = 1.0 / jnp.sum(jnp.where(is_c, jnp.exp(lg - cmax), 0.0), axis=-1, keepdims=True)
    lane_grp = ((lane_i - N_GROUPS) // EXPERTS_PER_GROUP).astype(F32)
    in_grp = (lane_i >= N_GROUPS) & (lane_i < N_GROUPS + N_EXPERTS) & (lane_grp == grp)
    f1 = jnp.where(in_grp, lg, neg)
    t1 = jnp.max(f1, axis=-1, keepdims=True)
    i1 = jnp.min(jnp.where(f1 == t1, lane, big), axis=-1, keepdims=True)
    f2 = jnp.where(lane == i1, neg, f1)
    t2 = jnp.max(f2, axis=-1, keepdims=True)
    i2 = jnp.min(jnp.where(f2 == t2, lane, big), axis=-1, keepdims=True)
    e2 = jnp.exp(t2 - t1)
    w1 = p_grp / (1.0 + e2)
    w2 = p_grp * e2 / (1.0 + e2)
    return jnp.where(lane == i1, w1, 0.0) + jnp.where(lane == i2, w2, 0.0)


def _moe_dense_kernel(x_ref, wrh_ref, wrl_ref, br_ref, wg_ref, wu_ref, wd_ref, lng_ref, lnb_ref, y_ref,
                      xb, gates, acc):
    e = pl.program_id(1)

    @pl.when(e == 0)
    def _():
        x = x_ref[...]
        xb[...] = x.astype(BF16)
        gates[...] = _router_gates(x, wrh_ref[...], wrl_ref[...], br_ref[...])
        acc[...] = jnp.zeros(acc.shape, F32)

    lane_i = lax.broadcasted_iota(jnp.int32, gates.shape, 1)
    gcol = jnp.sum(jnp.where(lane_i == e + N_GROUPS, gates[...], 0.0), axis=-1, keepdims=True)
    xv = xb[...]
    hid = _silu(_dot(xv, wg_ref[0])) * _dot(xv, wu_ref[0]) * gcol
    acc[...] += _dot(hid.astype(BF16), wd_ref[0])

    @pl.when(e == N_EXPERTS - 1)
    def _():
        y_ref[...] = _ln(DEEPNORM_ALPHA * x_ref[...] + acc[...], lng_ref[...], lnb_ref[...])


def _moe_dense(x2d, wr_hi, wr_lo, br, w_gate, w_up, w_down, ln_g, ln_b):
    n, d = x2d.shape
    tm = min(1024, n)
    xs = pl.BlockSpec((tm, d), lambda i, e: (i, 0))
    return pl.pallas_call(
        _moe_dense_kernel, out_shape=jax.ShapeDtypeStruct((n, d), F32), grid=(n // tm, N_EXPERTS),
        in_specs=[xs, _full(wr_hi.shape), _full(wr_lo.shape), _full(br.shape),
                  pl.BlockSpec((1, d, EXPERT_HIDDEN), lambda i, e: (e, 0, 0)),
                  pl.BlockSpec((1, d, EXPERT_HIDDEN), lambda i, e: (e, 0, 0)),
                  pl.BlockSpec((1, EXPERT_HIDDEN, d), lambda i, e: (e, 0, 0)),
                  _full(ln_g.shape), _full(ln_b.shape)],
        out_specs=xs,
        scratch_shapes=[pltpu.VMEM((tm, d), BF16), pltpu.VMEM((tm, ROUTER_LANES), F32), pltpu.VMEM((tm, d), F32)],
        compiler_params=_cparams("arbitrary", "arbitrary"), name="moe_dense",
    )(x2d, wr_hi, wr_lo, br, w_gate, w_up, w_down, ln_g, ln_b)


def _log_sigmoid(z):
    return jnp.minimum(z, 0.0) - jnp.log1p(jnp.exp(-jnp.abs(z)))


FOX_AUG = 2 * FOX_HEADS * FOX_HEAD_DIM
FOX_TILE = 512


def _fox_aug_tables():
    pq = np.zeros((128, FOX_AUG), np.float32)
    pk = np.zeros((128, FOX_AUG), np.float32)
    cq = np.zeros((1, FOX_AUG), np.float32)
    ck = np.zeros((1, FOX_AUG), np.float32)
    cv = np.zeros((1, FOX_AUG), np.float32)
    dm = np.zeros((1, FOX_AUG), np.float32)
    for h in range(FOX_HEADS):
        base = 128 * h
        off = base + (FOX_HEAD_DIM if h % 2 == 0 else 0)
        data = base + (0 if h % 2 == 0 else FOX_HEAD_DIM)
        dm[0, data:data + FOX_HEAD_DIM] = 1.0
        for g in range(3):
            pq[16 * g + h, off + g] = 1.0
            cq[0, off + 3 + g] = 1.0
            ck[0, off + g] = 1.0
            pk[16 * g + h, off + 3 + g] = -1.0
        cv[0, off] = 1.0
    return pq, pk, cq, ck, cv.T.copy(), dm, dm.T.copy()


def _fox_proj_kernel(x_ref, w_ref, wvt_ref, wf_ref, bf_ref, tri_ref, pq_ref, pk_ref, cq_ref, ck_ref, cvt_ref,
                     dm_ref, dmt_ref, qa_ref, ka_ref, vat_ref, k_ref, v_ref, lf_ref, carry):
    j = pl.program_id(1)

    @pl.when(j == 0)
    def _():
        carry[...] = jnp.zeros(carry.shape, F32)

    xb = x_ref[0].astype(BF16)
    h = _dot(xb, w_ref[...])
    q2 = h[:, :D_MODEL] * (FOX_HEAD_DIM ** -0.5 * LOG2E)
    k = h[:, D_MODEL:2 * D_MODEL]
    k_ref[0] = k
    v_ref[0] = h[:, 2 * D_MODEL:]
    vt = _dot_nt(wvt_ref[...], xb)
    lf = _log_sigmoid(_dot(xb, wf_ref[...]) + bf_ref[...])
    lf_ref[0] = lf[:, :FOX_HEADS]
    hi, mid, lo = _split3(lf)
    tri = tri_ref[...]
    c = _dot(tri, hi) + (_dot(tri, mid) + _dot(tri, lo)) + carry[...]
    carry[...] = c[c.shape[0] - 1:, :]
    chi, cmid, clo = _split3(c * LOG2E)
    lane = lax.broadcasted_iota(jnp.int32, c.shape, 1)
    c3 = jnp.where(lane < 16, chi.astype(F32), jnp.where(lane < 32, cmid.astype(F32), clo.astype(F32))).astype(BF16)
    qb = _dot(c3, pq_ref[...]) + cq_ref[...]
    kb = _dot(c3, pk_ref[...]) + ck_ref[...]
    for hh in range(FOX_HEADS):
        blk = slice(128 * hh, 128 * hh + 128)
        src = slice(128 * (hh // 2), 128 * (hh // 2) + 128)
        is_data = dm_ref[:, blk] > 0.0
        qa_ref[0, :, blk] = jnp.where(is_data, q2[:, src], qb[:, blk]).astype(BF16)
        ka_ref[0, :, blk] = jnp.where(is_data, k[:, src], kb[:, blk]).astype(BF16)
        vat_ref[0, blk, :] = jnp.where(dmt_ref[blk, :] > 0.0, vt[src, :], cvt_ref[blk, :]).astype(BF16)


def _fox_proj(x, w_qkv, w_vt, w_f3, b_f3):
    b, s, d = x.shape
    tm = min(FOX_TILE, s)
    tri = jnp.asarray(np.tril(np.ones((tm, tm), np.float32)), dtype=BF16)
    pq, pk, cq, ck, cvt, dm, dmt = _fox_aug_tables()
    consts = [jnp.asarray(pq, dtype=BF16), jnp.asarray(pk, dtype=BF16), jnp.asarray(cq), jnp.asarray(ck),
              jnp.asarray(cvt), jnp.asarray(dm), jnp.asarray(dmt)]
    xs = pl.BlockSpec((1, tm, d), lambda bi, j: (bi, j, 0))
    augs = pl.BlockSpec((1, tm, FOX_AUG), lambda bi, j: (bi, j, 0))
    f32o = jax.ShapeDtypeStruct((b, s, d), F32)
    augo = jax.ShapeDtypeStruct((b, s, FOX_AUG), BF16)
    return pl.pallas_call(
        _fox_proj_kernel,
        out_shape=(augo, augo, jax.ShapeDtypeStruct((b, FOX_AUG, s), BF16), f32o, f32o,
                   jax.ShapeDtypeStruct((b, s, FOX_HEADS), F32)),
        grid=(b, s // tm),
        in_specs=[xs, _full(w_qkv.shape), _full(w_vt.shape), _full(w_f3.shape), _full(b_f3.shape), _full(tri.shape)]
        + [_full(c.shape) for c in consts],
        out_specs=(augs, augs, pl.BlockSpec((1, FOX_AUG, tm), lambda bi, j: (bi, 0, j)), xs, xs,
                   pl.BlockSpec((1, tm, FOX_HEADS), lambda bi, j: (bi, j, 0))),
        scratch_shapes=[pltpu.VMEM((1, 128), F32)],
        compiler_params=_cparams("arbitrary", "arbitrary"), name="fox_proj",
    )(x, w_qkv, w_vt, w_f3, b_f3, tri, *consts)


def _fox_attn_kernel(tq, q_ref, k_ref, vt_ref, o_ref, m_s, acc_s):
    i = pl.program_id(2)
    krow = lax.broadcasted_iota(jnp.int32, (tq, tq), 0)
    qcol = lax.broadcasted_iota(jnp.int32, (tq, tq), 1)
    m_s[...] = jnp.full(m_s.shape, -jnp.inf, F32)
    acc_s[...] = jnp.zeros(acc_s.shape, F32)

    def tile(jj, masked):
        k0 = pl.multiple_of(jj * tq, tq)
        for hd in range(2):
            lanes = slice(128 * hd, 128 * hd + 128)
            st = _dot_nt(k_ref[0, pl.ds(k0, tq), lanes], q_ref[0, :, lanes])
            if masked:
                st = jnp.where(krow <= qcol, st, -jnp.inf)
            m_old = m_s[hd]
            m_new = jnp.maximum(m_old, jnp.max(st, axis=0, keepdims=True))
            pt = jnp.exp2(st - m_new).astype(BF16)
            acc_s[hd] = jnp.exp2(m_old - m_new) * acc_s[hd] + _dot(vt_ref[0, lanes, pl.ds(k0, tq)], pt)
            m_s[hd] = m_new

    def body(jj, carry):
        tile(jj, False)
        return carry

    lax.fori_loop(0, i, body, 0)
    tile(i, True)
    outs = []
    for hd in range(2):
        acc = acc_s[hd]
        l_row = FOX_HEAD_DIM if hd == 0 else 0
        outs.append(acc / acc[l_row:l_row + 1, :])
    row = lax.broadcasted_iota(jnp.int32, (128, tq), 0)
    o_ref[0] = jnp.where(row < FOX_HEAD_DIM, outs[0], outs[1]).T.astype(BF16)


def _fox_attn(qa, ka, vat):
    b, s, _ = qa.shape
    tq = min(FOX_TILE, s)
    qs = pl.BlockSpec((1, tq, 256), lambda bi, hp, i: (bi, i, hp))
    ks = pl.BlockSpec((1, s, 256), lambda bi, hp, i: (bi, 0, hp))
    vs = pl.BlockSpec((1, 256, s), lambda bi, hp, i: (bi, hp, 0))
    return pl.pallas_call(
        functools.partial(_fox_attn_kernel, tq),
        out_shape=jax.ShapeDtypeStruct((b, s, D_MODEL), BF16), grid=(b, FOX_HEADS // 2, s // tq),
        in_specs=[qs, ks, vs], out_specs=pl.BlockSpec((1, tq, 128), lambda bi, hp, i: (bi, i, hp)),
        scratch_shapes=[pltpu.VMEM((2, 1, tq), F32), pltpu.VMEM((2, 128, tq), F32)],
        compiler_params=_cparams("arbitrary", "arbitrary", "arbitrary"), name="fox_attn",
    )(qa, ka, vat)


def _proj_ln_kernel(o_ref, x_ref, w_ref, lng_ref, lnb_ref, y_ref):
    mix = _dot(o_ref[...].astype(BF16), w_ref[...])
    y_ref[...] = _ln(DEEPNORM_ALPHA * x_ref[...] + mix, lng_ref[...], lnb_ref[...])


def _proj_ln(o2d, x2d, w, ln_g, ln_b):
    n, d = x2d.shape
    tm = min(1024, n)
    xs = pl.BlockSpec((tm, d), lambda i: (i, 0))
    return pl.pallas_call(
        _proj_ln_kernel, out_shape=jax.ShapeDtypeStruct((n, d), F32), grid=(n // tm,),
        in_specs=[xs, xs, _full(w.shape), _full(ln_g.shape), _full(ln_b.shape)], out_specs=xs,
        compiler_params=_cparams("arbitrary"), name="proj_ln",
    )(o2d, x2d, w, ln_g, ln_b)


def _fox_proj_sample_kernel(x_ref, w_ref, wf_ref, bf_ref, q_ref, k_ref, v_ref, lf_ref):
    xb = x_ref[...].astype(BF16)
    h = _dot(xb, w_ref[...])
    q_ref[...] = h[:, :D_MODEL] * (FOX_HEAD_DIM ** -0.5)
    k_ref[...] = h[:, D_MODEL:2 * D_MODEL]
    v_ref[...] = h[:, 2 * D_MODEL:]
    lf_ref[...] = _log_sigmoid(_dot(xb, wf_ref[...]) + bf_ref[...])


def _fox_proj_sample(x2d, w_qkv, w_f, b_f):
    n, d = x2d.shape
    o = jax.ShapeDtypeStruct((n, d), F32)
    return pl.pallas_call(
        _fox_proj_sample_kernel, out_shape=(o, o, o, jax.ShapeDtypeStruct((n, 128), F32)),
        name="fox_proj_sample", compiler_params=pltpu.CompilerParams(vmem_limit_bytes=VMEM_LIMIT),
    )(x2d, w_qkv, w_f, b_f)


PAGES_PER_STEP = 8


def _fox_paged_kernel(npp, pt_ref, q_ref, kn_ref, vn_ref, lfn_ref, hm_ref, ut_ref, *rest):
    del pt_ref
    k_refs = rest[:npp]
    v_refs = rest[npp:2 * npp]
    lf_refs = rest[2 * npp:3 * npp]
    o_ref = rest[3 * npp]
    qb, m_s, l_s, acc_s, ccar = rest[3 * npp + 1:]
    s = pl.program_id(1)
    ns = pl.num_programs(1)
    nh, hd = FOX_HEADS, FOX_HEAD_DIM

    @pl.when(s == 0)
    def _():
        qb[...] = jnp.broadcast_to(q_ref[0], (128, D_MODEL)).T
        m_s[...] = jnp.full(m_s.shape, -jnp.inf, F32)
        l_s[...] = jnp.zeros(l_s.shape, F32)
        acc_s[...] = jnp.zeros(acc_s.shape, F32)
        ccar[...] = jnp.zeros(ccar.shape, F32)

    lf = jnp.concatenate([r[0] for r in lf_refs], axis=0)
    hi, mid, lo = _split3(lf)
    ut = ut_ref[...]
    cs_tot = _dot(hi, ut) + (_dot(mid, ut) + _dot(lo, ut))
    base = ccar[...]
    svals = []
    for p in range(npp):
        rows = slice(p * nh, (p + 1) * nh)
        c_p = cs_tot[rows, :128] + base
        base = base + cs_tot[rows, 128:]
        prod = k_refs[p][0] * qb[...]
        svals.append(jnp.sum(prod.reshape(nh, hd, 128), axis=1) - c_p)
    ccar[...] = base

    smax = svals[0]
    for p in range(1, npp):
        smax = jnp.maximum(smax, svals[p])
    m_old = m_s[...]
    m_new = jnp.maximum(m_old, jnp.max(smax, axis=-1, keepdims=True))
    alpha = jnp.exp(m_old - m_new)
    ps = [jnp.exp(sv - m_new) for sv in svals]
    psum = ps[0]
    for p in range(1, npp):
        psum = psum + ps[p]
    l_s[...] = alpha * l_s[...] + jnp.sum(psum, axis=-1, keepdims=True)
    m_s[...] = m_new

    def rows64(x):
        return jnp.concatenate([jnp.broadcast_to(x[h:h + 1, :], (hd, 128)) for h in range(nh)], axis=0)

    contrib = rows64(ps[0]) * v_refs[0][0]
    for p in range(1, npp):
        contrib = contrib + rows64(ps[p]) * v_refs[p][0]
    acc_s[...] = acc_s[...] * rows64(alpha) + contrib

    @pl.when(s == ns - 1)
    def _():
        hm = hm_ref[...]
        lane = lax.broadcasted_iota(jnp.int32, (nh, 128), 1)
        head = lax.broadcasted_iota(jnp.int32, (nh, 128), 0)
        lf_col = jnp.sum(jnp.where(lane == head, jnp.broadcast_to(lfn_ref[0], (nh, 128)), 0.0),
                         axis=-1, keepdims=True)
        qk = jnp.broadcast_to(q_ref[0] * kn_ref[0], (nh, D_MODEL)) * hm
        s_new = jnp.sum(qk, axis=-1, keepdims=True) - (ccar[...] + lf_col)
        m_prev = m_s[...]
        m_fin = jnp.maximum(m_prev, s_new)
        a2 = jnp.exp(m_prev - m_fin)
        pn = jnp.exp(s_new - m_fin)
        l_fin = a2 * l_s[...] + pn
        vnb = jnp.broadcast_to(vn_ref[0], (128, D_MODEL)).T
        lane_w = lax.broadcasted_iota(jnp.int32, (D_MODEL, 128), 1)
        tot = acc_s[...] * rows64(a2) + jnp.where(lane_w == 0, rows64(pn) * vnb, 0.0)
        o_col = jnp.sum(tot, axis=-1, keepdims=True) / rows64(l_fin)
        o_ref[0] = o_col.T[0:1, :]


def _fox_paged(q, k_new, v_new, lf_new, cache_kt, cache_vt, cache_lft, page_table):
    b, d = q.shape
    n_pages = page_table.shape[1]
    npp = PAGES_PER_STEP
    hm = np.zeros((FOX_HEADS, d), np.float32)
    for h in range(FOX_HEADS):
        hm[h, h * FOX_HEAD_DIM:(h + 1) * FOX_HEAD_DIM] = 1.0
    ut = np.concatenate([np.triu(np.ones((128, 128), np.float32)), np.ones((128, 128), np.float32)], axis=1)
    row = lambda w: pl.BlockSpec((1, 1, w), lambda bi, s, pt: (bi, 0, 0))
    const = lambda shape: pl.BlockSpec(shape, lambda bi, s, pt: (0,) * len(shape))

    def page_spec(p, r):
        return pl.BlockSpec((1, r, PAGE_SIZE), lambda bi, s, pt: (pt[bi * n_pages + s * npp + p], 0, 0))

    in_specs = ([row(d), row(d), row(d), row(128), const((FOX_HEADS, d)), const((128, 256))]
                + [page_spec(p, d) for p in range(npp)] + [page_spec(p, d) for p in range(npp)]
                + [page_spec(p, FOX_HEADS) for p in range(npp)])
    grid_spec = pltpu.PrefetchScalarGridSpec(
        num_scalar_prefetch=1, grid=(b, n_pages // npp), in_specs=in_specs, out_specs=row(d),
        scratch_shapes=[pltpu.VMEM((d, 128), F32), pltpu.VMEM((FOX_HEADS, 128), F32),
                        pltpu.VMEM((FOX_HEADS, 128), F32), pltpu.VMEM((d, 128), F32),
                        pltpu.VMEM((FOX_HEADS, 128), F32)])
    out = pl.pallas_call(
        functools.partial(_fox_paged_kernel, npp), out_shape=jax.ShapeDtypeStruct((b, 1, d), F32),
        grid_spec=grid_spec, compiler_params=_cparams("arbitrary", "arbitrary"), name="fox_paged",
    )(page_table.reshape(-1), q[:, None, :], k_new[:, None, :], v_new[:, None, :], lf_new[:, None, :],
      jnp.asarray(hm), jnp.asarray(ut, dtype=BF16),
      *([cache_kt] * npp), *([cache_vt] * npp), *([cache_lft] * npp))
    return out[:, 0, :]


def _router_weights(w_gc, b_gc, w_gf, b_gf):
    d = w_gc.shape[0]
    wf = jnp.transpose(w_gf, (1, 0, 2)).reshape(d, N_EXPERTS)
    wr = jnp.zeros((d, ROUTER_LANES), F32).at[:, :N_GROUPS].set(w_gc).at[:, N_GROUPS:N_GROUPS + N_EXPERTS].set(wf)
    br = jnp.zeros((1, ROUTER_LANES), F32).at[0, :N_GROUPS].set(b_gc).at[0, N_GROUPS:N_GROUPS + N_EXPERTS].set(
        b_gf.reshape(-1))
    hi = wr.astype(BF16)
    lo = (wr - hi.astype(F32)).astype(BF16)
    return hi, lo, br


def kernel(x_prompt, x_sample, mem_prompt, state_pool, state_ret, cache_fox_k, cache_fox_v, cache_fox_logf,
           cache_mem_k, cache_mem_v, page_table, ln_g, ln_b, ab_w_in, pool_w, pool_scale, ret_gn_g, ab_w_o,
           fox_w_in, fox_b_f, fox_w_o, mem_wq, mem_wkv, mem_wo, moe_w_gc, moe_b_gc, moe_w_gf, moe_b_gf,
           moe_w_up, moe_w_gate, moe_w_down):
    bp, s, d = x_prompt.shape
    bs = x_sample.shape[0]
    n_phys = cache_fox_k.shape[1]
    row = lambda a: a.reshape(1, -1)
    rows8 = lambda y2: jnp.broadcast_to(y2[:, None, :], (bs, 8, d))

    yp = x_prompt
    ys = x_sample.reshape(bs, d)
    mk_all, mv_all = _mem_kv(mem_prompt.reshape(bp * N_MEM, d), mem_wkv.astype(BF16))

    perm = _qk_perm()
    inv_perm = np.argsort(perm)
    for l in range(DEPTH):
        g0, b0 = row(ln_g[l, 0]), row(ln_b[l, 0])
        if l % 2 == 0:
            e = l // 2
            cols = np.concatenate([np.arange(POOL_WIDTH), POOL_WIDTH + perm, POOL_WIDTH + RET_QK + perm,
                                   np.arange(POOL_WIDTH + 2 * RET_QK, ab_w_in.shape[-1])])
            w_in = ab_w_in[e][:, cols].astype(BF16)
            w_pool = pool_w[e].astype(BF16)
            w_o = ab_w_o[e].astype(BF16)
            yp, pool_p, ret_p = _ab_prompt(yp, w_in, w_pool, row(pool_scale[e]), row(ret_gn_g[e]), w_o, g0, b0)
            ret_prev = state_ret[e].reshape(bs, RET_QK, RET_V_DIM)[:, perm, :]
            y8, pool_s, ret_s = _ab_sample(rows8(ys), w_in, state_pool[e], ret_prev, w_pool, row(pool_scale[e]),
                                           row(ret_gn_g[e]), w_o, g0, b0)
            ys = y8[:, 0, :]
            new_pool_p = pool_p[:, POOL_HIST - POOL_STATE:, :][None]
            new_pool_s = pool_s[None]
            rp = ret_p[:, inv_perm, :].reshape(bp, RET_HEADS, RET_QK_DIM, RET_HEADS, RET_V_DIM)
            new_ret_p = jnp.stack([rp[:, h, :, h, :] for h in range(RET_HEADS)], axis=1)[None]
            new_ret_s = ret_s[:, inv_perm, :].reshape(bs, RET_HEADS, RET_QK_DIM, RET_V_DIM)[None]
        else:
            o = l // 2
            w = fox_w_in[o]
            w_qkv = w[:, :3 * d].astype(BF16)
            wf = w[:, 3 * d:]
            w_f3 = jnp.pad(jnp.concatenate([wf, wf, wf], axis=1), ((0, 0), (0, 128 - 3 * FOX_HEADS))).astype(BF16)
            bf = fox_b_f[o]
            b_f3 = jnp.pad(jnp.concatenate([bf, bf, bf]), (0, 128 - 3 * FOX_HEADS)).reshape(1, 128)
            w_o = fox_w_o[o].astype(BF16)
            w_vt = jnp.transpose(w[:, 2 * d:3 * d]).astype(BF16)
            qa, ka, vat, k, v, lf = _fox_proj(yp, w_qkv, w_vt, w_f3, b_f3)
            att = _fox_attn(qa, ka, vat)
            yp = _proj_ln(att.reshape(bp * s, d), yp.reshape(bp * s, d), w_o, g0, b0).reshape(bp, s, d)
            new_fk_p = k.reshape(1, bp, s, FOX_HEADS, FOX_HEAD_DIM)
            new_fv_p = v.reshape(1, bp, s, FOX_HEADS, FOX_HEAD_DIM)
            new_fl_p = lf[None]
            qs, ks, vs, lfs = _fox_proj_sample(ys, w_qkv, w_f3, b_f3)
            kt = jnp.transpose(cache_fox_k[o], (0, 2, 3, 1)).reshape(n_phys, d, PAGE_SIZE)
            vt = jnp.transpose(cache_fox_v[o], (0, 2, 3, 1)).reshape(n_phys, d, PAGE_SIZE)
            lft = jnp.transpose(cache_fox_logf[o], (0, 2, 1))
            att_s = _fox_paged(qs, ks, vs, lfs, kt, vt, lft, page_table)
            ys = _proj_ln(att_s, ys, w_o, g0, b0)
            new_fk_s = ks.reshape(1, bs, 1, FOX_HEADS, FOX_HEAD_DIM)
            new_fv_s = vs.reshape(1, bs, 1, FOX_HEADS, FOX_HEAD_DIM)
            new_fl_s = lfs[:, :FOX_HEADS].reshape(1, bs, 1, FOX_HEADS)

        g1, b1 = row(ln_g[l, 1]), row(ln_b[l, 1])
        wq = mem_wq[l].astype(BF16)
        wo = mem_wo[l].astype(BF16)
        yp = _mem_attn(yp, wq, mk_all[l].reshape(bp, N_MEM, d), mv_all[l].reshape(bp, N_MEM, d), wo, g1, b1)
        ys = _mem_attn(rows8(ys), wq, cache_mem_k[l].reshape(bs, N_MEM, d), cache_mem_v[l].reshape(bs, N_MEM, d),
                       wo, g1, b1)[:, 0, :]

        g2, b2 = row(ln_g[l, 2]), row(ln_b[l, 2])
        wr_hi, wr_lo, br = _router_weights(moe_w_gc[l], moe_b_gc[l], moe_w_gf[l], moe_b_gf[l])
        wg, wu, wd = moe_w_gate[l].astype(BF16), moe_w_up[l].astype(BF16), moe_w_down[l].astype(BF16)
        yp = _moe_dense(yp.reshape(bp * s, d), wr_hi, wr_lo, br, wg, wu, wd, g2, b2).reshape(bp, s, d)
        ys = _moe_dense(ys, wr_hi, wr_lo, br, wg, wu, wd, g2, b2)

    new_mk_p = mk_all.reshape(DEPTH, bp, N_MEM, MEM_HEADS, MEM_HEAD_DIM)
    new_mv_p = mv_all.reshape(DEPTH, bp, N_MEM, MEM_HEADS, MEM_HEAD_DIM)
    return (yp, ys.reshape(bs, 1, d), new_pool_p, new_pool_s, new_ret_p, new_ret_s, new_fk_p, new_fk_s,
            new_fv_p, new_fv_s, new_fl_p, new_fl_s, new_mk_p, new_mv_p)
```

```python
import functools

import numpy as np
import jax
import jax.numpy as jnp
from jax import lax
from jax.experimental import pallas as pl
from jax.experimental.pallas import tpu as pltpu

F32 = jnp.float32
BF16 = jnp.bfloat16

D_MODEL = 1024
DEPTH = 2
PAST_LEN = 8192
PAGE_SIZE = 128

POOL_WIDTH = 512
POOL_GROUP_DIM = 128
POOL_WINDOWS = (2, 4, 8, 16)
POOL_STATE = 15
POOL_HIST = 16

RET_HEADS = 4
RET_QK_DIM = 64
RET_V_DIM = 128
RET_QK = 256
RET_V = 512
RET_CHUNK = 128
ROPE_BASE = 10000.0
ROPE_HALF = RET_QK_DIM // 2

FOX_HEADS = 16
FOX_HEAD_DIM = 64

N_MEM = 256
MEM_HEADS = 4
MEM_HEAD_DIM = 256

N_GROUPS = 4
EXPERTS_PER_GROUP = 4
N_EXPERTS = 16
EXPERT_HIDDEN = 256
ROUTER_LANES = 128

DEEPNORM_ALPHA = (2 * DEPTH) ** 0.25
LN_EPS = 1e-5
LOG2E = 1.4426950408889634

VMEM_LIMIT = 56 * 1024 * 1024


def _cparams(*sem):
    return pltpu.CompilerParams(dimension_semantics=sem, vmem_limit_bytes=VMEM_LIMIT)


def _full(shape):
    n = len(shape)
    return pl.BlockSpec(shape, lambda *_: (0,) * n)


def _ln(z, g, b):
    mu = jnp.mean(z, axis=-1, keepdims=True)
    zc = z - mu
    var = jnp.mean(zc * zc, axis=-1, keepdims=True)
    return zc * lax.rsqrt(var + LN_EPS) * g + b


def _dot(a, b):
    return jnp.dot(a, b, preferred_element_type=F32)


def _dot_nt(a, b):
    return lax.dot_general(a, b, (((1,), (1,)), ((), ())), preferred_element_type=F32)


def _silu(x):
    return x * jax.nn.sigmoid(x)


def _split_bf16(x):
    hi = x.astype(BF16)
    lo = (x - hi.astype(F32)).astype(BF16)
    return hi, lo


def _split3(x):
    hi = x.astype(BF16)
    r1 = x - hi.astype(F32)
    mid = r1.astype(BF16)
    lo = (r1 - mid.astype(F32)).astype(BF16)
    return hi, mid, lo


def _qk_perm():
    p = np.arange(RET_QK)
    h = (p % 128) // ROPE_HALF
    d = (p % ROPE_HALF) + ROPE_HALF * (p // 128)
    return h * RET_QK_DIM + d


def _lane_head():
    return (np.arange(RET_QK) % 128) // ROPE_HALF


def _log_gamma():
    return np.log(1.0 - 2.0 ** (-5.0 - np.arange(RET_HEADS, dtype=np.float64)))


def _rope_tables(pos):
    inv = ROPE_BASE ** (-np.arange(ROPE_HALF, dtype=np.float64) / ROPE_HALF)
    ang = pos.astype(np.float64)[:, None] * inv[None, :]
    cos = np.tile(np.cos(ang).astype(np.float32), (1, RET_HEADS))
    sin = np.tile(np.sin(ang).astype(np.float32), (1, RET_HEADS))
    return cos, sin


def _ret_tables(chunk):
    lg = _log_gamma()
    lh = _lane_head()
    i = np.arange(chunk, dtype=np.float64)
    dist = i[:, None] - i[None, :]
    dmask = np.where(dist >= 0, np.exp(lg[:, None, None] * np.maximum(dist, 0.0)), 0.0)
    qdec = np.exp(lg[lh][None, :] * (i[:, None] + 1.0))
    kdec = np.exp(lg[lh][None, :] * (chunk - 1.0 - i[:, None]))
    sdec = np.repeat(np.exp(lg[lh] * chunk)[:, None], RET_V, axis=1)
    bmask = (lh[:, None] == (np.arange(RET_V) // RET_V_DIM)[None, :]).astype(np.float64)
    hmask = (lh[None, :] == np.arange(RET_HEADS)[:, None]).astype(np.float64)
    f = lambda a: jnp.asarray(a.astype(np.float32))
    return f(dmask), f(qdec), f(kdec), f(sdec), f(bmask), f(hmask)


def _rotary_cat(x, cos, sin):
    x1, x2 = x[:, :128], x[:, 128:]
    return jnp.concatenate([x1 * cos - x2 * sin, x1 * sin + x2 * cos], axis=1)


def _head_norm(o, gn):
    parts = []
    for h in range(RET_HEADS):
        oh = o[:, h * RET_V_DIM:(h + 1) * RET_V_DIM]
        mu = jnp.mean(oh, axis=-1, keepdims=True)
        oc = oh - mu
        var = jnp.mean(oc * oc, axis=-1, keepdims=True)
        parts.append(oc * lax.rsqrt(var + LN_EPS))
    return jnp.concatenate(parts, axis=1) * gn


def _ab_prompt_kernel(tm, x_ref, win_ref, cos_ref, sin_ref, dmask_ref, qdec_ref, kdec_ref, sdec_ref, bmask_ref,
                      hmask_ref, wpool_ref, spool_ref, gn_ref, wo_ref, lng_ref, lnb_ref,
                      y_ref, pool_ref, ret_ref, uext, state, cat):
    j = pl.program_id(1)

    @pl.when(j == 0)
    def _():
        uext[0:POOL_HIST, :] = jnp.zeros((POOL_HIST, POOL_WIDTH), F32)
        state[...] = jnp.zeros(state.shape, F32)

    x = x_ref[0]
    h = _dot(x.astype(BF16), win_ref[...])
    u = h[:, :POOL_WIDTH]
    uext[POOL_HIST:POOL_HIST + tm, :] = u

    pos = j * tm + lax.broadcasted_iota(jnp.int32, (tm, 1), 0)
    for g, w in enumerate(POOL_WINDOWS):
        lanes = slice(g * POOL_GROUP_DIM, (g + 1) * POOL_GROUP_DIM)
        win = u[:, lanes]
        for s in range(1, w):
            win = win + uext[POOL_HIST - s:POOL_HIST - s + tm, lanes]
        cnt = jnp.minimum(w, pos + 1).astype(F32)
        diff = win / cnt - u[:, lanes]
        mixed = _dot(diff.astype(BF16), wpool_ref[g]) * spool_ref[:, lanes]
        cat[:, lanes] = mixed.astype(BF16)
    tail = uext[tm:tm + POOL_HIST, :]
    uext[0:POOL_HIST, :] = tail
    pool_ref[0] = tail

    cos = cos_ref[...]
    sin = sin_ref[...]
    qr = _rotary_cat(h[:, POOL_WIDTH:POOL_WIDTH + RET_QK], cos, sin)
    kr = _rotary_cat(h[:, POOL_WIDTH + RET_QK:POOL_WIDTH + 2 * RET_QK], cos, sin) * (RET_QK_DIM ** -0.5)
    v = h[:, POOL_WIDTH + 2 * RET_QK:POOL_WIDTH + 2 * RET_QK + RET_V]
    gate = h[:, POOL_WIDTH + 2 * RET_QK + RET_V:]
    gn = gn_ref[...]
    c = RET_CHUNK
    for ci in range(tm // c):
        rows = slice(ci * c, (ci + 1) * c)
        qc, kc = qr[rows], kr[rows]
        kcb = kc.astype(BF16)
        vcb = v[rows].astype(BF16)
        sm = state[...]
        o = _dot((qc * qdec_ref[...]).astype(BF16), sm.astype(BF16))
        parts = []
        for hh in range(RET_HEADS):
            qm = (qc * hmask_ref[hh:hh + 1, :]).astype(BF16)
            sc = _dot_nt(qm, kcb) * dmask_ref[hh]
            parts.append(_dot(sc.astype(BF16), vcb[:, hh * RET_V_DIM:(hh + 1) * RET_V_DIM]))
        o = o + jnp.concatenate(parts, axis=1)
        kdt = (kc * kdec_ref[...]).T.astype(BF16)
        state[...] = sm * sdec_ref[...] + _dot(kdt, vcb) * bmask_ref[...]
        cat[rows, POOL_WIDTH:] = (_silu(gate[rows]) * _head_norm(o, gn)).astype(BF16)
    ret_ref[0] = state[...]

    mix = _dot(cat[...], wo_ref[...])
    y_ref[0] = _ln(DEEPNORM_ALPHA * x + mix, lng_ref[...], lnb_ref[...])


def _ab_prompt(x, w_in, w_pool, s_pool, gn_g, w_o, ln_g, ln_b):
    b, s, d = x.shape
    tm = min(512, s)
    cos, sin = _rope_tables(np.arange(s))
    tabs = _ret_tables(RET_CHUNK)
    nin = w_in.shape[1]
    in_specs = [
        pl.BlockSpec((1, tm, d), lambda bi, j: (bi, j, 0)),
        _full((d, nin)),
        pl.BlockSpec((tm, 128), lambda bi, j: (j, 0)),
        pl.BlockSpec((tm, 128), lambda bi, j: (j, 0)),
    ] + [_full(t.shape) for t in tabs] + [
        _full(w_pool.shape), _full(s_pool.shape), _full(gn_g.shape), _full(w_o.shape),
        _full(ln_g.shape), _full(ln_b.shape),
    ]
    out_shape = (jax.ShapeDtypeStruct((b, s, d), F32),
                 jax.ShapeDtypeStruct((b, POOL_HIST, POOL_WIDTH), F32),
                 jax.ShapeDtypeStruct((b, RET_QK, RET_V), F32))
    out_specs = (pl.BlockSpec((1, tm, d), lambda bi, j: (bi, j, 0)),
                 pl.BlockSpec((1, POOL_HIST, POOL_WIDTH), lambda bi, j: (bi, 0, 0)),
                 pl.BlockSpec((1, RET_QK, RET_V), lambda bi, j: (bi, 0, 0)))
    return pl.pallas_call(
        functools.partial(_ab_prompt_kernel, tm),
        out_shape=out_shape, grid=(b, s // tm), in_specs=in_specs, out_specs=out_specs,
        scratch_shapes=[pltpu.VMEM((POOL_HIST + tm, POOL_WIDTH), F32),
                        pltpu.VMEM((RET_QK, RET_V), F32),
                        pltpu.VMEM((tm, d), BF16)],
        compiler_params=_cparams("arbitrary", "arbitrary"), name="ab_prompt",
    )(x, w_in, jnp.asarray(cos), jnp.asarray(sin), *tabs, w_pool, s_pool, gn_g, w_o, ln_g, ln_b)


def _ab_sample_kernel(x_ref, win_ref, cos_ref, sin_ref, hmask_ref, gdec_ref, pool_ref, ret_ref,
                      wpool_ref, spool_ref, gn_ref, wo_ref, lng_ref, lnb_ref,
                      y_ref, npool_ref, nret_ref):
    x = x_ref[0]
    h = _dot(x.astype(BF16), win_ref[...])
    u = h[:, :POOL_WIDTH]
    prev = pool_ref[0]
    npool_ref[0, 0:POOL_STATE - 1, :] = prev[1:POOL_STATE, :]
    npool_ref[0, POOL_STATE - 1:POOL_STATE, :] = u[0:1, :]
    mixed = []
    for g, w in enumerate(POOL_WINDOWS):
        lanes = slice(g * POOL_GROUP_DIM, (g + 1) * POOL_GROUP_DIM)
        win = u[0:1, lanes] + jnp.sum(prev[POOL_STATE - (w - 1):, lanes], axis=0, keepdims=True)
        diff = win / float(w) - u[0:1, lanes]
        diff8 = jnp.broadcast_to(diff, (8, POOL_GROUP_DIM))
        mixed.append(_dot(diff8.astype(BF16), wpool_ref[g]) * spool_ref[:, lanes])
    a_out = jnp.concatenate(mixed, axis=1)

    cos = cos_ref[...]
    sin = sin_ref[...]
    qr = _rotary_cat(h[:, POOL_WIDTH:POOL_WIDTH + RET_QK], cos, sin)[0:1]
    kr = (_rotary_cat(h[:, POOL_WIDTH + RET_QK:POOL_WIDTH + 2 * RET_QK], cos, sin) * (RET_QK_DIM ** -0.5))[0:1]
    v = h[0:1, POOL_WIDTH + 2 * RET_QK:POOL_WIDTH + 2 * RET_QK + RET_V]
    gate = h[0:1, POOL_WIDTH + 2 * RET_QK + RET_V:]
    hm = hmask_ref[...]
    gdec = gdec_ref[...]
    sp = ret_ref[0]
    q4 = qr * hm
    k4 = kr * hm
    sc = jnp.sum(q4 * kr, axis=-1, keepdims=True)
    v4 = jnp.concatenate([v[:, hh * RET_V_DIM:(hh + 1) * RET_V_DIM] for hh in range(RET_HEADS)]
                         + [jnp.zeros((8 - RET_HEADS, RET_V_DIM), F32)], axis=0)
    o4 = sc.astype(BF16).astype(F32) * v4.astype(BF16).astype(F32) \
        + _dot((q4 * gdec).astype(BF16), sp.astype(BF16))
    kv = lax.dot_general(k4.astype(BF16), v4.astype(BF16), (((0,), (0,)), ((), ())), preferred_element_type=F32)
    gcol = jnp.sum(hm * gdec, axis=0, keepdims=True)
    nret_ref[0] = sp * gcol.T + kv
    mu = jnp.mean(o4, axis=-1, keepdims=True)
    oc = o4 - mu
    var = jnp.mean(oc * oc, axis=-1, keepdims=True)
    on = oc * lax.rsqrt(var + LN_EPS)
    on_row = jnp.concatenate([on[hh:hh + 1, :] for hh in range(RET_HEADS)], axis=1) * gn_ref[...]
    b_out = jnp.broadcast_to(_silu(gate) * on_row, (8, RET_V))
    cat = jnp.concatenate([a_out, b_out], axis=1).astype(BF16)
    mix = _dot(cat, wo_ref[...])
    y_ref[0] = _ln(DEEPNORM_ALPHA * x + mix, lng_ref[...], lnb_ref[...])


def _ab_sample(x8, w_in, pool_prev, ret_prev_perm, w_pool, s_pool, gn_g, w_o, ln_g, ln_b):
    b, _, d = x8.shape
    cos, sin = _rope_tables(np.array([PAST_LEN]))
    hmask = np.zeros((8, RET_QK), np.float32)
    hmask[:RET_HEADS] = (_lane_head()[None, :] == np.arange(RET_HEADS)[:, None])
    gdec = np.exp(_log_gamma())[_lane_head()][None, :].astype(np.float32)
    per_b = lambda shape: pl.BlockSpec((1,) + shape, lambda i: (i, 0, 0))
    in_specs = [per_b((8, d)), _full(w_in.shape), _full((1, 128)), _full((1, 128)), _full((8, RET_QK)),
                _full((1, RET_QK)), per_b((POOL_STATE, POOL_WIDTH)), per_b((RET_QK, RET_V_DIM)),
                _full(w_pool.shape), _full(s_pool.shape), _full(gn_g.shape), _full(w_o.shape),
                _full(ln_g.shape), _full(ln_b.shape)]
    out_shape = (jax.ShapeDtypeStruct((b, 8, d), F32),
                 jax.ShapeDtypeStruct((b, POOL_STATE, POOL_WIDTH), F32),
                 jax.ShapeDtypeStruct((b, RET_QK, RET_V_DIM), F32))
    out_specs = (per_b((8, d)), per_b((POOL_STATE, POOL_WIDTH)), per_b((RET_QK, RET_V_DIM)))
    return pl.pallas_call(
        _ab_sample_kernel, out_shape=out_shape, grid=(b,), in_specs=in_specs, out_specs=out_specs,
        compiler_params=_cparams("arbitrary"), name="ab_sample",
    )(x8, w_in, jnp.asarray(cos), jnp.asarray(sin), jnp.asarray(hmask), jnp.asarray(gdec),
      pool_prev, ret_prev_perm, w_pool, s_pool, gn_g, w_o, ln_g, ln_b)


def _mem_kv_kernel(m_ref, w_ref, k_ref, v_ref):
    h = _dot(m_ref[...].astype(BF16), w_ref[0])
    k_ref[0] = h[:, :D_MODEL]
    v_ref[0] = h[:, D_MODEL:]


def _mem_kv(mem2d, w_kv):
    n, d = mem2d.shape
    depth = w_kv.shape[0]
    out = jax.ShapeDtypeStruct((depth, n, d), F32)
    return pl.pallas_call(
        _mem_kv_kernel, out_shape=(out, out), grid=(depth,),
        in_specs=[_full((n, d)), pl.BlockSpec((1, d, 2 * d), lambda l: (l, 0, 0))],
        out_specs=(pl.BlockSpec((1, n, d), lambda l: (l, 0, 0)), pl.BlockSpec((1, n, d), lambda l: (l, 0, 0))),
        compiler_params=_cparams("arbitrary"), name="mem_kv",
    )(mem2d, w_kv)


def _mem_attn_kernel(x_ref, wq_ref, k_ref, v_ref, wo_ref, lng_ref, lnb_ref, y_ref):
    x = x_ref[0]
    q = _dot(x.astype(BF16), wq_ref[...]) * (MEM_HEAD_DIM ** -0.5)
    parts = []
    for h in range(MEM_HEADS):
        lanes = slice(h * MEM_HEAD_DIM, (h + 1) * MEM_HEAD_DIM)
        logits = _dot_nt(q[:, lanes].astype(BF16), k_ref[0, :, lanes].astype(BF16))
        m = jnp.max(logits, axis=-1, keepdims=True)
        p = jnp.exp(logits - m)
        l = jnp.sum(p, axis=-1, keepdims=True)
        parts.append(_dot(p.astype(BF16), v_ref[0, :, lanes].astype(BF16)) / l)
    o = jnp.concatenate(parts, axis=1)
    mix = _dot(o.astype(BF16), wo_ref[...])
    y_ref[0] = _ln(DEEPNORM_ALPHA * x + mix, lng_ref[...], lnb_ref[...])


def _mem_attn(x, w_q, k, v, w_o, ln_g, ln_b):
    b, s, d = x.shape
    tm = min(512, s)
    xs = pl.BlockSpec((1, tm, d), lambda bi, j: (bi, j, 0))
    kvs = pl.BlockSpec((1, N_MEM, d), lambda bi, j: (bi, 0, 0))
    return pl.pallas_call(
        _mem_attn_kernel, out_shape=jax.ShapeDtypeStruct((b, s, d), F32), grid=(b, s // tm),
        in_specs=[xs, _full(w_q.shape), kvs, kvs, _full(w_o.shape), _full(ln_g.shape), _full(ln_b.shape)],
        out_specs=xs, compiler_params=_cparams("arbitrary", "arbitrary"), name="mem_attn",
    )(x, w_q, k, v, w_o, ln_g, ln_b)


def _mem_attn_sample_kernel(x_ref, wq_ref, k_ref, v_ref, wo_ref, lng_ref, lnb_ref, y_ref, q_s, o_s):
    b = pl.program_id(0)
    nb = pl.num_programs(0)

    @pl.when(b == 0)
    def _():
        q_s[...] = _dot(x_ref[...].astype(BF16), wq_ref[...]) * (MEM_HEAD_DIM ** -0.5)

    qrow = q_s[pl.ds(b, 1), :]
    q4 = jnp.concatenate([qrow[:, h * MEM_HEAD_DIM:(h + 1) * MEM_HEAD_DIM] for h in range(MEM_HEADS)]
                         + [jnp.zeros((8 - MEM_HEADS, MEM_HEAD_DIM), F32)], axis=0)
    k2 = k_ref[0, 0].reshape(N_MEM * MEM_HEADS, MEM_HEAD_DIM).astype(BF16)
    v2 = v_ref[0, 0].reshape(N_MEM * MEM_HEADS, MEM_HEAD_DIM).astype(BF16)
    logits = _dot_nt(q4.astype(BF16), k2)
    col = lax.broadcasted_iota(jnp.int32, logits.shape, 1)
    rowi = lax.broadcasted_iota(jnp.int32, logits.shape, 0)
    own = (col % MEM_HEADS) == rowi
    m = jnp.max(jnp.where(own, logits, -jnp.inf), axis=-1, keepdims=True)
    m = jnp.where(rowi[:, :1] < MEM_HEADS, m, 0.0)
    p = jnp.where(own, jnp.exp(logits - m), 0.0)
    l = jnp.maximum(jnp.sum(p, axis=-1, keepdims=True), jnp.where(rowi[:, :1] < MEM_HEADS, 0.0, 1.0))
    o4 = _dot(p.astype(BF16), v2) / l
    o_s[pl.ds(b, 1), :] = jnp.concatenate([o4[h:h + 1, :] for h in range(MEM_HEADS)], axis=1)

    @pl.when(b == nb - 1)
    def _():
        mix = _dot(o_s[...].astype(BF16), wo_ref[...])
        y_ref[...] = _ln(DEEPNORM_ALPHA * x_ref[...] + mix, lng_ref[...], lnb_ref[...])


def _mem_attn_sample(x2d, w_q, cache_k, cache_v, layer, w_o, ln_g, ln_b):
    n, d = x2d.shape
    kvs = pl.BlockSpec((1, 1, N_MEM, MEM_HEADS, MEM_HEAD_DIM), lambda b: (layer, b, 0, 0, 0))
    return pl.pallas_call(
        _mem_attn_sample_kernel, out_shape=jax.ShapeDtypeStruct((n, d), F32), grid=(n,),
        in_specs=[_full((n, d)), _full(w_q.shape), kvs, kvs, _full(w_o.shape), _full(ln_g.shape), _full(ln_b.shape)],
        out_specs=_full((n, d)),
        scratch_shapes=[pltpu.VMEM((n, d), F32), pltpu.VMEM((n, d), F32)],
        compiler_params=_cparams("arbitrary"), name="mem_attn_sample",
    )(x2d, w_q, cache_k, cache_v, w_o, ln_g, ln_b)


def _router_gates(x, wr_hi, wr_lo, br):
    xh, xl = _split_bf16(x)
    lg = _dot(xh, wr_hi) + (_dot(xh, wr_lo) + _dot(xl, wr_hi)) + br
    n = x.shape[0]
    lane_i = lax.broadcasted_iota(jnp.int32, (n, ROUTER_LANES), 1)
    lane = lane_i.astype(F32)
    is_c = lane_i < N_GROUPS
    neg = jnp.float32(-jnp.inf)
    big = jnp.float32(ROUTER_LANES)
    coarse = jnp.where(is_c, lg, neg)
    cmax = jnp.max(coarse, axis=-1, keepdims=True)
    grp = jnp.min(jnp.where(coarse == cmax, lane, big), axis=-1, keepdims=True)
    p_grp = 1.0 / jnp.sum(jnp.where(is_c, jnp.exp(lg - cmax), 0.0), axis=-1, keepdims=True)
    lane_grp = ((lane_i - N_GROUPS) // EXPERTS_PER_GROUP).astype(F32)
    in_grp = (lane_i >= N_GROUPS) & (lane_i < N_GROUPS + N_EXPERTS) & (lane_grp == grp)
    f1 = jnp.where(in_grp, lg, neg)
    t1 = jnp.max(f1, axis=-1, keepdims=True)
    i1 = jnp.min(jnp.where(f1 == t1, lane, big), axis=-1, keepdims=True)
    f2 = jnp.where(lane == i1, neg, f1)
    t2 = jnp.max(f2, axis=-1, keepdims=True)
    i2 = jnp.min(jnp.where(f2 == t2, lane, big), axis=-1, keepdims=True)
    e2 = jnp.exp(t2 - t1)
    w1 = p_grp / (1.0 + e2)
    w2 = p_grp * e2 / (1.0 + e2)
    return jnp.where(lane == i1, w1, 0.0) + jnp.where(lane == i2, w2, 0.0)


def _moe_dense_kernel(x_ref, wrh_ref, wrl_ref, br_ref, wg_ref, wu_ref, wd_ref, lng_ref, lnb_ref, y_ref,
                      xb, gates, acc):
    e = pl.program_id(1)

    @pl.when(e == 0)
    def _():
        x = x_ref[...]
        xb[...] = x.astype(BF16)
        gates[...] = _router_gates(x, wrh_ref[...], wrl_ref[...], br_ref[...])
        acc[...] = jnp.zeros(acc.shape, F32)

    lane_i = lax.broadcasted_iota(jnp.int32, gates.shape, 1)
    gcol = jnp.sum(jnp.where(lane_i == e + N_GROUPS, gates[...], 0.0), axis=-1, keepdims=True)
    xv = xb[...]
    hid = _silu(_dot(xv, wg_ref[0])) * _dot(xv, wu_ref[0]) * gcol
    acc[...] += _dot(hid.astype(BF16), wd_ref[0])

    @pl.when(e == N_EXPERTS - 1)
    def _():
        y_ref[...] = _ln(DEEPNORM_ALPHA * x_ref[...] + acc[...], lng_ref[...], lnb_ref[...])


def _moe_dense(x2d, wr_hi, wr_lo, br, w_gate, w_up, w_down, ln_g, ln_b):
    n, d = x2d.shape
    tm = min(1024, n)
    xs = pl.BlockSpec((tm, d), lambda i, e: (i, 0))
    return pl.pallas_call(
        _moe_dense_kernel, out_shape=jax.ShapeDtypeStruct((n, d), F32), grid=(n // tm, N_EXPERTS),
        in_specs=[xs, _full(wr_hi.shape), _full(wr_lo.shape), _full(br.shape),
                  pl.BlockSpec((1, d, EXPERT_HIDDEN), lambda i, e: (e, 0, 0)),
                  pl.BlockSpec((1, d, EXPERT_HIDDEN), lambda i, e: (e, 0, 0)),
                  pl.BlockSpec((1, EXPERT_HIDDEN, d), lambda i, e: (e, 0, 0)),
                  _full(ln_g.shape), _full(ln_b.shape)],
        out_specs=xs,
        scratch_shapes=[pltpu.VMEM((tm, d), BF16), pltpu.VMEM((tm, ROUTER_LANES), F32), pltpu.VMEM((tm, d), F32)],
        compiler_params=_cparams("arbitrary", "arbitrary"), name="moe_dense",
    )(x2d, wr_hi, wr_lo, br, w_gate, w_up, w_down, ln_g, ln_b)


def _log_sigmoid(z):
    return jnp.minimum(z, 0.0) - jnp.log1p(jnp.exp(-jnp.abs(z)))


FOX_AUG = 2 * FOX_HEADS * FOX_HEAD_DIM
FOX_TILE = 512


def _fox_aug_tables():
    pq = np.zeros((128, FOX_AUG), np.float32)
    pk = np.zeros((128, FOX_AUG), np.float32)
    cq = np.zeros((1, FOX_AUG), np.float32)
    ck = np.zeros((1, FOX_AUG), np.float32)
    cv = np.zeros((1, FOX_AUG), np.float32)
    dm = np.zeros((1, FOX_AUG), np.float32)
    for h in range(FOX_HEADS):
        base = 128 * h
        off = base + (FOX_HEAD_DIM if h % 2 == 0 else 0)
        data = base + (0 if h % 2 == 0 else FOX_HEAD_DIM)
        dm[0, data:data + FOX_HEAD_DIM] = 1.0
        for g in range(3):
            pq[16 * g + h, off + g] = 1.0
            cq[0, off + 3 + g] = 1.0
            ck[0, off + g] = 1.0
            pk[16 * g + h, off + 3 + g] = -1.0
        cv[0, off] = 1.0
    return pq, pk, cq, ck, cv.T.copy(), dm, dm.T.copy()


def _fox_proj_kernel(x_ref, w_ref, wvt_ref, wf_ref, bf_ref, tri_ref, pq_ref, pk_ref, cq_ref, ck_ref, cvt_ref,
                     dm_ref, dmt_ref, qa_ref, ka_ref, vat_ref, k_ref, v_ref, lf_ref, carry):
    j = pl.program_id(1)

    @pl.when(j == 0)
    def _():
        carry[...] = jnp.zeros(carry.shape, F32)

    xb = x_ref[0].astype(BF16)
    h = _dot(xb, w_ref[...])
    q2 = h[:, :D_MODEL] * (FOX_HEAD_DIM ** -0.5 * LOG2E)
    k = h[:, D_MODEL:2 * D_MODEL]
    k_ref[0] = k
    v_ref[0] = h[:, 2 * D_MODEL:]
    vt = _dot_nt(wvt_ref[...], xb)
    lf = _log_sigmoid(_dot(xb, wf_ref[...]) + bf_ref[...])
    lf_ref[0] = lf[:, :FOX_HEADS]
    hi, mid, lo = _split3(lf)
    tri = tri_ref[...]
    c = _dot(tri, hi) + (_dot(tri, mid) + _dot(tri, lo)) + carry[...]
    carry[...] = c[c.shape[0] - 1:, :]
    chi, cmid, clo = _split3(c * LOG2E)
    lane = lax.broadcasted_iota(jnp.int32, c.shape, 1)
    c3 = jnp.where(lane < 16, chi.astype(F32), jnp.where(lane < 32, cmid.astype(F32), clo.astype(F32))).astype(BF16)
    qb = _dot(c3, pq_ref[...]) + cq_ref[...]
    kb = _dot(c3, pk_ref[...]) + ck_ref[...]
    for hh in range(FOX_HEADS):
        blk = slice(128 * hh, 128 * hh + 128)
        src = slice(128 * (hh // 2), 128 * (hh // 2) + 128)
        is_data = dm_ref[:, blk] > 0.0
        qa_ref[0, :, blk] = jnp.where(is_data, q2[:, src], qb[:, blk]).astype(BF16)
        ka_ref[0, :, blk] = jnp.where(is_data, k[:, src], kb[:, blk]).astype(BF16)
        vat_ref[0, blk, :] = jnp.where(dmt_ref[blk, :] > 0.0, vt[src, :], cvt_ref[blk, :]).astype(BF16)


def _fox_proj(x, w_qkv, w_vt, w_f3, b_f3):
    b, s, d = x.shape
    tm = min(FOX_TILE, s)
    tri = jnp.asarray(np.tril(np.ones((tm, tm), np.float32)), dtype=BF16)
    pq, pk, cq, ck, cvt, dm, dmt = _fox_aug_tables()
    consts = [jnp.asarray(pq, dtype=BF16), jnp.asarray(pk, dtype=BF16), jnp.asarray(cq), jnp.asarray(ck),
              jnp.asarray(cvt), jnp.asarray(dm), jnp.asarray(dmt)]
    xs = pl.BlockSpec((1, tm, d), lambda bi, j: (bi, j, 0))
    augs = pl.BlockSpec((1, tm, FOX_AUG), lambda bi, j: (bi, j, 0))
    f32o = jax.ShapeDtypeStruct((b, s, d), F32)
    augo = jax.ShapeDtypeStruct((b, s, FOX_AUG), BF16)
    return pl.pallas_call(
        _fox_proj_kernel,
        out_shape=(augo, augo, jax.ShapeDtypeStruct((b, FOX_AUG, s), BF16), f32o, f32o,
                   jax.ShapeDtypeStruct((b, s, FOX_HEADS), F32)),
        grid=(b, s // tm),
        in_specs=[xs, _full(w_qkv.shape), _full(w_vt.shape), _full(w_f3.shape), _full(b_f3.shape), _full(tri.shape)]
        + [_full(c.shape) for c in consts],
        out_specs=(augs, augs, pl.BlockSpec((1, FOX_AUG, tm), lambda bi, j: (bi, 0, j)), xs, xs,
                   pl.BlockSpec((1, tm, FOX_HEADS), lambda bi, j: (bi, j, 0))),
        scratch_shapes=[pltpu.VMEM((1, 128), F32)],
        compiler_params=_cparams("arbitrary", "arbitrary"), name="fox_proj",
    )(x, w_qkv, w_vt, w_f3, b_f3, tri, *consts)


def _fox_attn_body(tq, i, q_ref, k_ref, vt_ref, o_ref, m_s, acc_s, s_buf):
    krow = lax.broadcasted_iota(jnp.int32, (tq, tq), 0)
    qcol = lax.broadcasted_iota(jnp.int32, (tq, tq), 1)
    m_s[...] = jnp.full(m_s.shape, -jnp.inf, F32)
    acc_s[...] = jnp.zeros(acc_s.shape, F32)
    heads = [slice(128 * hd, 128 * hd + 128) for hd in range(2)]

    def scores(jj, diagonal):
        k0 = pl.multiple_of(jj * tq, tq)
        sts = [_dot_nt(k_ref[0, pl.ds(k0, tq), lanes], q_ref[0, :, lanes]) for lanes in heads]
        if diagonal:
            sts = [jnp.where(krow <= qcol, st, -jnp.inf) for st in sts]
        return sts

    def consume(jj):
        k0 = pl.multiple_of(jj * tq, tq)
        for hd, lanes in enumerate(heads):
            st = s_buf[hd]
            m_old = m_s[hd]
            m_new = jnp.maximum(m_old, jnp.max(st, axis=0, keepdims=True))
            pt = jnp.exp2(st - m_new).astype(BF16)
            acc_s[hd] = jnp.exp2(m_old - m_new) * acc_s[hd] + _dot(vt_ref[0, lanes, pl.ds(k0, tq)], pt)
            m_s[hd] = m_new

    def step(jj, diagonal_next):
        nxt = scores(jj + 1, diagonal_next)
        consume(jj)
        for hd in range(2):
            s_buf[hd] = nxt[hd]

    @pl.when(i == 0)
    def _():
        for hd, st in enumerate(scores(0, True)):
            s_buf[hd] = st

    @pl.when(i > 0)
    def _():
        for hd, st in enumerate(scores(0, False)):
            s_buf[hd] = st

        def body(jj, carry):
            step(jj, False)
            return carry

        lax.fori_loop(0, i - 1, body, 0)
        step(i - 1, True)

    consume(i)
    outs = []
    for hd in range(2):
        acc = acc_s[hd]
        l_row = FOX_HEAD_DIM if hd == 0 else 0
        outs.append(acc / acc[l_row:l_row + 1, :])
    row = lax.broadcasted_iota(jnp.int32, (128, tq), 0)
    o_ref[0] = jnp.where(row < FOX_HEAD_DIM, outs[0], outs[1]).T.astype(BF16)


def _proj_ln_kernel(o_ref, x_ref, w_ref, lng_ref, lnb_ref, y_ref):
    mix = _dot(o_ref[...].astype(BF16), w_ref[...])
    y_ref[...] = _ln(DEEPNORM_ALPHA * x_ref[...] + mix, lng_ref[...], lnb_ref[...])


def _proj_ln(o2d, x2d, w, ln_g, ln_b):
    n, d = x2d.shape
    tm = min(1024, n)
    xs = pl.BlockSpec((tm, d), lambda i: (i, 0))
    return pl.pallas_call(
        _proj_ln_kernel, out_shape=jax.ShapeDtypeStruct((n, d), F32), grid=(n // tm,),
        in_specs=[xs, xs, _full(w.shape), _full(ln_g.shape), _full(ln_b.shape)], out_specs=xs,
        compiler_params=_cparams("arbitrary"), name="proj_ln",
    )(o2d, x2d, w, ln_g, ln_b)


def _fox_proj_sample_kernel(x_ref, w_ref, wf_ref, bf_ref, q_ref, k_ref, v_ref, lf_ref):
    xb = x_ref[...].astype(BF16)
    h = _dot(xb, w_ref[...])
    q_ref[...] = h[:, :D_MODEL] * (FOX_HEAD_DIM ** -0.5)
    k_ref[...] = h[:, D_MODEL:2 * D_MODEL]
    v_ref[...] = h[:, 2 * D_MODEL:]
    lf_ref[...] = _log_sigmoid(_dot(xb, wf_ref[...]) + bf_ref[...])


def _fox_proj_sample(x2d, w_qkv, w_f, b_f):
    n, d = x2d.shape
    o = jax.ShapeDtypeStruct((n, d), F32)
    return pl.pallas_call(
        _fox_proj_sample_kernel, out_shape=(o, o, o, jax.ShapeDtypeStruct((n, 128), F32)),
        name="fox_proj_sample", compiler_params=pltpu.CompilerParams(vmem_limit_bytes=VMEM_LIMIT),
    )(x2d, w_qkv, w_f, b_f)


PAGES_PER_STEP = 8


def _fox_paged_body(s, ns, q_ref, kn_ref, vn_ref, lfn_ref, hm_ref, ut_ref, k_refs, v_refs, lf_refs, o_ref,
                    qb, m_s, l_s, acc_s, ccar):
    npp = len(k_refs)
    nh, hd = FOX_HEADS, FOX_HEAD_DIM

    @pl.when(s == 0)
    def _():
        qb[...] = jnp.broadcast_to(q_ref[0], (128, D_MODEL)).T
        m_s[...] = jnp.full(m_s.shape, -jnp.inf, F32)
        l_s[...] = jnp.zeros(l_s.shape, F32)
        acc_s[...] = jnp.zeros(acc_s.shape, F32)
        ccar[...] = jnp.zeros(ccar.shape, F32)

    lf = jnp.concatenate([r[0] for r in lf_refs], axis=0)
    hi, mid, lo = _split3(lf)
    ut = ut_ref[...]
    cs_tot = _dot(hi, ut) + (_dot(mid, ut) + _dot(lo, ut))
    base = ccar[...]
    svals = []
    for p in range(npp):
        rows = slice(p * nh, (p + 1) * nh)
        c_p = cs_tot[rows, :128] + base
        base = base + cs_tot[rows, 128:]
        prod = k_refs[p][0] * qb[...]
        svals.append(jnp.sum(prod.reshape(nh, hd, 128), axis=1) - c_p)
    ccar[...] = base

    smax = svals[0]
    for p in range(1, npp):
        smax = jnp.maximum(smax, svals[p])
    m_old = m_s[...]
    m_new = jnp.maximum(m_old, jnp.max(smax, axis=-1, keepdims=True))
    alpha = jnp.exp(m_old - m_new)
    ps = [jnp.exp(sv - m_new) for sv in svals]
    psum = ps[0]
    for p in range(1, npp):
        psum = psum + ps[p]
    l_s[...] = alpha * l_s[...] + jnp.sum(psum, axis=-1, keepdims=True)
    m_s[...] = m_new

    def rows64(x):
        return jnp.concatenate([jnp.broadcast_to(x[h:h + 1, :], (hd, 128)) for h in range(nh)], axis=0)

    contrib = rows64(ps[0]) * v_refs[0][0]
    for p in range(1, npp):
        contrib = contrib + rows64(ps[p]) * v_refs[p][0]
    acc_s[...] = acc_s[...] * rows64(alpha) + contrib

    @pl.when(s == ns - 1)
    def _():
        hm = hm_ref[...]
        lane = lax.broadcasted_iota(jnp.int32, (nh, 128), 1)
        head = lax.broadcasted_iota(jnp.int32, (nh, 128), 0)
        lf_col = jnp.sum(jnp.where(lane == head, jnp.broadcast_to(lfn_ref[0], (nh, 128)), 0.0),
                         axis=-1, keepdims=True)
        qk = jnp.broadcast_to(q_ref[0] * kn_ref[0], (nh, D_MODEL)) * hm
        s_new = jnp.sum(qk, axis=-1, keepdims=True) - (ccar[...] + lf_col)
        m_prev = m_s[...]
        m_fin = jnp.maximum(m_prev, s_new)
        a2 = jnp.exp(m_prev - m_fin)
        pn = jnp.exp(s_new - m_fin)
        l_fin = a2 * l_s[...] + pn
        vnb = jnp.broadcast_to(vn_ref[0], (128, D_MODEL)).T
        lane_w = lax.broadcasted_iota(jnp.int32, (D_MODEL, 128), 1)
        tot = acc_s[...] * rows64(a2) + jnp.where(lane_w == 0, rows64(pn) * vnb, 0.0)
        o_col = jnp.sum(tot, axis=-1, keepdims=True) / rows64(l_fin)
        o_ref[0] = o_col.T[0:1, :]


def _fox_attn_paged_kernel(tq, npp, pt_ref, qa_ref, ka_ref, vat_ref, q_ref, kn_ref, vn_ref, lfn_ref, hm_ref, ut_ref,
                           *rest):
    del pt_ref
    k_refs, v_refs, lf_refs = rest[:npp], rest[npp:2 * npp], rest[2 * npp:3 * npp]
    o_att, o_smp = rest[3 * npp:3 * npp + 2]
    m_a, acc_a, s_buf, qb, m_p, l_p, acc_p, ccar = rest[3 * npp + 2:]
    i = pl.program_id(2)
    _fox_attn_body(tq, i, qa_ref, ka_ref, vat_ref, o_att, m_a, acc_a, s_buf)
    _fox_paged_body(i, pl.num_programs(2), q_ref, kn_ref, vn_ref, lfn_ref, hm_ref, ut_ref, k_refs, v_refs, lf_refs,
                    o_smp, qb, m_p, l_p, acc_p, ccar)


def _fox_attn_paged(qa, ka, vat, q, k_new, v_new, lf_new, cache_kt, cache_vt, cache_lft, page_table):
    bp, s, _ = qa.shape
    bs, d = q.shape
    tq = min(FOX_TILE, s)
    nq = s // tq
    n_pages = page_table.shape[1]
    npp = PAGES_PER_STEP
    hm = np.zeros((FOX_HEADS, d), np.float32)
    for h in range(FOX_HEADS):
        hm[h, h * FOX_HEAD_DIM:(h + 1) * FOX_HEAD_DIM] = 1.0
    ut = np.concatenate([np.triu(np.ones((128, 128), np.float32)), np.ones((128, 128), np.float32)], axis=1)
    assert bs == bp * (FOX_HEADS // 2) and n_pages == npp * nq, (bs, bp, n_pages, nq)
    seq = lambda bi, hp: bi * (FOX_HEADS // 2) + hp
    row = lambda w: pl.BlockSpec((1, 1, w), lambda bi, hp, i, pt: (seq(bi, hp), 0, 0))
    const = lambda shape: pl.BlockSpec(shape, lambda bi, hp, i, pt: (0,) * len(shape))

    def page_spec(p, r):
        return pl.BlockSpec((1, r, PAGE_SIZE),
                            lambda bi, hp, i, pt: (pt[seq(bi, hp) * n_pages + i * npp + p], 0, 0))

    qs = pl.BlockSpec((1, tq, 256), lambda bi, hp, i, pt: (bi, i, hp))
    in_specs = ([qs, pl.BlockSpec((1, s, 256), lambda bi, hp, i, pt: (bi, 0, hp)),
                 pl.BlockSpec((1, 256, s), lambda bi, hp, i, pt: (bi, hp, 0)),
                 row(d), row(d), row(d), row(128), const((FOX_HEADS, d)), const((128, 256))]
                + [page_spec(p, d) for p in range(npp)] + [page_spec(p, d) for p in range(npp)]
                + [page_spec(p, FOX_HEADS) for p in range(npp)])
    grid_spec = pltpu.PrefetchScalarGridSpec(
        num_scalar_prefetch=1, grid=(bp, FOX_HEADS // 2, nq), in_specs=in_specs,
        out_specs=(pl.BlockSpec((1, tq, 128), lambda bi, hp, i, pt: (bi, i, hp)), row(d)),
        scratch_shapes=[pltpu.VMEM((2, 1, tq), F32), pltpu.VMEM((2, 128, tq), F32), pltpu.VMEM((2, tq, tq), F32),
                        pltpu.VMEM((d, 128), F32), pltpu.VMEM((FOX_HEADS, 128), F32),
                        pltpu.VMEM((FOX_HEADS, 128), F32), pltpu.VMEM((d, 128), F32),
                        pltpu.VMEM((FOX_HEADS, 128), F32)])
    att, out = pl.pallas_call(
        functools.partial(_fox_attn_paged_kernel, tq, npp),
        out_shape=(jax.ShapeDtypeStruct((bp, s, D_MODEL), BF16), jax.ShapeDtypeStruct((bs, 1, d), F32)),
        grid_spec=grid_spec, compiler_params=_cparams("arbitrary", "arbitrary", "arbitrary"), name="fox_attn_paged",
    )(page_table.reshape(-1), qa, ka, vat, q[:, None, :], k_new[:, None, :], v_new[:, None, :], lf_new[:, None, :],
      jnp.asarray(hm), jnp.asarray(ut, dtype=BF16),
      *([cache_kt] * npp), *([cache_vt] * npp), *([cache_lft] * npp))
    return att, out[:, 0, :]


def _router_weights(w_gc, b_gc, w_gf, b_gf):
    d = w_gc.shape[0]
    wf = jnp.transpose(w_gf, (1, 0, 2)).reshape(d, N_EXPERTS)
    wr = jnp.zeros((d, ROUTER_LANES), F32).at[:, :N_GROUPS].set(w_gc).at[:, N_GROUPS:N_GROUPS + N_EXPERTS].set(wf)
    br = jnp.zeros((1, ROUTER_LANES), F32).at[0, :N_GROUPS].set(b_gc).at[0, N_GROUPS:N_GROUPS + N_EXPERTS].set(
        b_gf.reshape(-1))
    hi = wr.astype(BF16)
    lo = (wr - hi.astype(F32)).astype(BF16)
    return hi, lo, br


def kernel(x_prompt, x_sample, mem_prompt, state_pool, state_ret, cache_fox_k, cache_fox_v, cache_fox_logf,
           cache_mem_k, cache_mem_v, page_table, ln_g, ln_b, ab_w_in, pool_w, pool_scale, ret_gn_g, ab_w_o,
           fox_w_in, fox_b_f, fox_w_o, mem_wq, mem_wkv, mem_wo, moe_w_gc, moe_b_gc, moe_w_gf, moe_b_gf,
           moe_w_up, moe_w_gate, moe_w_down):
    bp, s, d = x_prompt.shape
    bs = x_sample.shape[0]
    n_phys = cache_fox_k.shape[1]
    row = lambda a: a.reshape(1, -1)
    rows8 = lambda y2: jnp.broadcast_to(y2[:, None, :], (bs, 8, d))

    yp = x_prompt
    ys = x_sample.reshape(bs, d)
    mk_all, mv_all = _mem_kv(mem_prompt.reshape(bp * N_MEM, d), mem_wkv.astype(BF16))

    perm = _qk_perm()
    inv_perm = np.argsort(perm)
    for l in range(DEPTH):
        g0, b0 = row(ln_g[l, 0]), row(ln_b[l, 0])
        if l % 2 == 0:
            e = l // 2
            cols = np.concatenate([np.arange(POOL_WIDTH), POOL_WIDTH + perm, POOL_WIDTH + RET_QK + perm,
                                   np.arange(POOL_WIDTH + 2 * RET_QK, ab_w_in.shape[-1])])
            w_in = ab_w_in[e][:, cols].astype(BF16)
            w_pool = pool_w[e].astype(BF16)
            w_o = ab_w_o[e].astype(BF16)
            yp, pool_p, ret_p = _ab_prompt(yp, w_in, w_pool, row(pool_scale[e]), row(ret_gn_g[e]), w_o, g0, b0)
            ret_prev = state_ret[e].reshape(bs, RET_QK, RET_V_DIM)[:, perm, :]
            y8, pool_s, ret_s = _ab_sample(rows8(ys), w_in, state_pool[e], ret_prev, w_pool, row(pool_scale[e]),
                                           row(ret_gn_g[e]), w_o, g0, b0)
            ys = y8[:, 0, :]
            new_pool_p = pool_p[:, POOL_HIST - POOL_STATE:, :][None]
            new_pool_s = pool_s[None]
            rp = ret_p[:, inv_perm, :].reshape(bp, RET_HEADS, RET_QK_DIM, RET_HEADS, RET_V_DIM)
            new_ret_p = jnp.stack([rp[:, h, :, h, :] for h in range(RET_HEADS)], axis=1)[None]
            new_ret_s = ret_s[:, inv_perm, :].reshape(bs, RET_HEADS, RET_QK_DIM, RET_V_DIM)[None]
        else:
            o = l // 2
            w = fox_w_in[o]
            w_qkv = w[:, :3 * d].astype(BF16)
            wf = w[:, 3 * d:]
            w_f3 = jnp.pad(jnp.concatenate([wf, wf, wf], axis=1), ((0, 0), (0, 128 - 3 * FOX_HEADS))).astype(BF16)
            bf = fox_b_f[o]
            b_f3 = jnp.pad(jnp.concatenate([bf, bf, bf]), (0, 128 - 3 * FOX_HEADS)).reshape(1, 128)
            w_o = fox_w_o[o].astype(BF16)
            w_vt = jnp.transpose(w[:, 2 * d:3 * d]).astype(BF16)
            qa, ka, vat, k, v, lf = _fox_proj(yp, w_qkv, w_vt, w_f3, b_f3)
            qs, ks, vs, lfs = _fox_proj_sample(ys, w_qkv, w_f3, b_f3)
            kt = jnp.transpose(cache_fox_k[o], (0, 2, 3, 1)).reshape(n_phys, d, PAGE_SIZE)
            vt = jnp.transpose(cache_fox_v[o], (0, 2, 3, 1)).reshape(n_phys, d, PAGE_SIZE)
            lft = jnp.transpose(cache_fox_logf[o], (0, 2, 1))
            att, att_s = _fox_attn_paged(qa, ka, vat, qs, ks, vs, lfs, kt, vt, lft, page_table)
            yp = _proj_ln(att.reshape(bp * s, d), yp.reshape(bp * s, d), w_o, g0, b0).reshape(bp, s, d)
            new_fk_p = k.reshape(1, bp, s, FOX_HEADS, FOX_HEAD_DIM)
            new_fv_p = v.reshape(1, bp, s, FOX_HEADS, FOX_HEAD_DIM)
            new_fl_p = lf[None]
            ys = _proj_ln(att_s, ys, w_o, g0, b0)
            new_fk_s = ks.reshape(1, bs, 1, FOX_HEADS, FOX_HEAD_DIM)
            new_fv_s = vs.reshape(1, bs, 1, FOX_HEADS, FOX_HEAD_DIM)
            new_fl_s = lfs[:, :FOX_HEADS].reshape(1, bs, 1, FOX_HEADS)

        g1, b1 = row(ln_g[l, 1]), row(ln_b[l, 1])
        wq = mem_wq[l].astype(BF16)
        wo = mem_wo[l].astype(BF16)
        yp = _mem_attn(yp, wq, mk_all[l].reshape(bp, N_MEM, d), mv_all[l].reshape(bp, N_MEM, d), wo, g1, b1)
        ys = _mem_attn_sample(ys, wq, cache_mem_k, cache_mem_v, l, wo, g1, b1)

        g2, b2 = row(ln_g[l, 2]), row(ln_b[l, 2])
        wr_hi, wr_lo, br = _router_weights(moe_w_gc[l], moe_b_gc[l], moe_w_gf[l], moe_b_gf[l])
        wg, wu, wd = moe_w_gate[l].astype(BF16), moe_w_up[l].astype(BF16), moe_w_down[l].astype(BF16)
        yp = _moe_dense(yp.reshape(bp * s, d), wr_hi, wr_lo, br, wg, wu, wd, g2, b2).reshape(bp, s, d)
        ys = _moe_dense(ys, wr_hi, wr_lo, br, wg, wu, wd, g2, b2)

    new_mk_p = mk_all.reshape(DEPTH, bp, N_MEM, MEM_HEADS, MEM_HEAD_DIM)
    new_mv_p = mv_all.reshape(DEPTH, bp, N_MEM, MEM_HEADS, MEM_HEAD_DIM)
    return (yp, ys.reshape(bs, 1, d), new_pool_p, new_pool_s, new_ret_p, new_ret_s, new_fk_p, new_fk_s,
            new_fv_p, new_fv_s, new_fl_p, new_fl_s, new_mk_p, new_mv_p)
```

```python
import functools

import numpy as np
import jax
import jax.numpy as jnp
from jax import lax
from jax.experimental import pallas as pl
from jax.experimental.pallas import tpu as pltpu

F32 = jnp.float32
BF16 = jnp.bfloat16

D_MODEL = 1024
DEPTH = 2
PAST_LEN = 8192
PAGE_SIZE = 128

POOL_WIDTH = 512
POOL_GROUP_DIM = 128
POOL_WINDOWS = (2, 4, 8, 16)
POOL_STATE = 15
POOL_HIST = 16

RET_HEADS = 4
RET_QK_DIM = 64
RET_V_DIM = 128
RET_QK = 256
RET_V = 512
RET_CHUNK = 128
ROPE_BASE = 10000.0
ROPE_HALF = RET_QK_DIM // 2

FOX_HEADS = 16
FOX_HEAD_DIM = 64

N_MEM = 256
MEM_HEADS = 4
MEM_HEAD_DIM = 256

N_GROUPS = 4
EXPERTS_PER_GROUP = 4
N_EXPERTS = 16
EXPERT_HIDDEN = 256
ROUTER_LANES = 128

DEEPNORM_ALPHA = (2 * DEPTH) ** 0.25
LN_EPS = 1e-5
LOG2E = 1.4426950408889634

VMEM_LIMIT = 56 * 1024 * 1024


def _cparams(*sem):
    return pltpu.CompilerParams(dimension_semantics=sem, vmem_limit_bytes=VMEM_LIMIT)


def _full(shape):
    n = len(shape)
    return pl.BlockSpec(shape, lambda *_: (0,) * n)


def _ln(z, g, b):
    mu = jnp.mean(z, axis=-1, keepdims=True)
    zc = z - mu
    var = jnp.mean(zc * zc, axis=-1, keepdims=True)
    return zc * lax.rsqrt(var + LN_EPS) * g + b


def _dot(a, b):
    return jnp.dot(a, b, preferred_element_type=F32)


def _dot_nt(a, b):
    return lax.dot_general(a, b, (((1,), (1,)), ((), ())), preferred_element_type=F32)


def _silu(x):
    return x * jax.nn.sigmoid(x)


def _split_bf16(x):
    hi = x.astype(BF16)
    lo = (x - hi.astype(F32)).astype(BF16)
    return hi, lo


def _split3(x):
    hi = x.astype(BF16)
    r1 = x - hi.astype(F32)
    mid = r1.astype(BF16)
    lo = (r1 - mid.astype(F32)).astype(BF16)
    return hi, mid, lo


def _qk_perm():
    p = np.arange(RET_QK)
    h = (p % 128) // ROPE_HALF
    d = (p % ROPE_HALF) + ROPE_HALF * (p // 128)
    return h * RET_QK_DIM + d


def _lane_head():
    return (np.arange(RET_QK) % 128) // ROPE_HALF


def _log_gamma():
    return np.log(1.0 - 2.0 ** (-5.0 - np.arange(RET_HEADS, dtype=np.float64)))


def _rope_tables(pos):
    inv = ROPE_BASE ** (-np.arange(ROPE_HALF, dtype=np.float64) / ROPE_HALF)
    ang = pos.astype(np.float64)[:, None] * inv[None, :]
    cos = np.tile(np.cos(ang).astype(np.float32), (1, RET_HEADS))
    sin = np.tile(np.sin(ang).astype(np.float32), (1, RET_HEADS))
    return cos, sin


def _ret_tables(chunk):
    lg = _log_gamma()
    lh = _lane_head()
    i = np.arange(chunk, dtype=np.float64)
    dist = i[:, None] - i[None, :]
    dmask = np.where(dist >= 0, np.exp(lg[:, None, None] * np.maximum(dist, 0.0)), 0.0)
    qdec = np.exp(lg[lh][None, :] * (i[:, None] + 1.0))
    kdec = np.exp(lg[lh][None, :] * (chunk - 1.0 - i[:, None]))
    sdec = np.repeat(np.exp(lg[lh] * chunk)[:, None], RET_V, axis=1)
    bmask = (lh[:, None] == (np.arange(RET_V) // RET_V_DIM)[None, :]).astype(np.float64)
    hmask = (lh[None, :] == np.arange(RET_HEADS)[:, None]).astype(np.float64)
    f = lambda a: jnp.asarray(a.astype(np.float32))
    return f(dmask), f(qdec), f(kdec), f(sdec), f(bmask), f(hmask)


def _rotary_cat(x, cos, sin):
    x1, x2 = x[:, :128], x[:, 128:]
    return jnp.concatenate([x1 * cos - x2 * sin, x1 * sin + x2 * cos], axis=1)


def _head_norm(o, gn):
    parts = []
    for h in range(RET_HEADS):
        oh = o[:, h * RET_V_DIM:(h + 1) * RET_V_DIM]
        mu = jnp.mean(oh, axis=-1, keepdims=True)
        oc = oh - mu
        var = jnp.mean(oc * oc, axis=-1, keepdims=True)
        parts.append(oc * lax.rsqrt(var + LN_EPS))
    return jnp.concatenate(parts, axis=1) * gn


def _ab_prompt_kernel(tm, x_ref, win_ref, cos_ref, sin_ref, dmask_ref, qdec_ref, kdec_ref, sdec_ref, bmask_ref,
                      hmask_ref, wpool_ref, spool_ref, gn_ref, wo_ref, lng_ref, lnb_ref,
                      y_ref, pool_ref, ret_ref, uext, state, cat):
    j = pl.program_id(1)

    @pl.when(j == 0)
    def _():
        uext[0:POOL_HIST, :] = jnp.zeros((POOL_HIST, POOL_WIDTH), F32)
        state[...] = jnp.zeros(state.shape, F32)

    x = x_ref[0]
    h = _dot(x.astype(BF16), win_ref[...])
    u = h[:, :POOL_WIDTH]
    uext[POOL_HIST:POOL_HIST + tm, :] = u

    pos = j * tm + lax.broadcasted_iota(jnp.int32, (tm, 1), 0)
    for g, w in enumerate(POOL_WINDOWS):
        lanes = slice(g * POOL_GROUP_DIM, (g + 1) * POOL_GROUP_DIM)
        win = u[:, lanes]
        for s in range(1, w):
            win = win + uext[POOL_HIST - s:POOL_HIST - s + tm, lanes]
        cnt = jnp.minimum(w, pos + 1).astype(F32)
        diff = win / cnt - u[:, lanes]
        mixed = _dot(diff.astype(BF16), wpool_ref[g]) * spool_ref[:, lanes]
        cat[:, lanes] = mixed.astype(BF16)
    tail = uext[tm:tm + POOL_HIST, :]
    uext[0:POOL_HIST, :] = tail
    pool_ref[0] = tail

    cos = cos_ref[...]
    sin = sin_ref[...]
    qr = _rotary_cat(h[:, POOL_WIDTH:POOL_WIDTH + RET_QK], cos, sin)
    kr = _rotary_cat(h[:, POOL_WIDTH + RET_QK:POOL_WIDTH + 2 * RET_QK], cos, sin) * (RET_QK_DIM ** -0.5)
    v = h[:, POOL_WIDTH + 2 * RET_QK:POOL_WIDTH + 2 * RET_QK + RET_V]
    gate = h[:, POOL_WIDTH + 2 * RET_QK + RET_V:]
    gn = gn_ref[...]
    c = RET_CHUNK
    for ci in range(tm // c):
        rows = slice(ci * c, (ci + 1) * c)
        qc, kc = qr[rows], kr[rows]
        kcb = kc.astype(BF16)
        vcb = v[rows].astype(BF16)
        sm = state[...]
        o = _dot((qc * qdec_ref[...]).astype(BF16), sm.astype(BF16))
        parts = []
        for hh in range(RET_HEADS):
            qm = (qc * hmask_ref[hh:hh + 1, :]).astype(BF16)
            sc = _dot_nt(qm, kcb) * dmask_ref[hh]
            parts.append(_dot(sc.astype(BF16), vcb[:, hh * RET_V_DIM:(hh + 1) * RET_V_DIM]))
        o = o + jnp.concatenate(parts, axis=1)
        kdt = (kc * kdec_ref[...]).T.astype(BF16)
        state[...] = sm * sdec_ref[...] + _dot(kdt, vcb) * bmask_ref[...]
        cat[rows, POOL_WIDTH:] = (_silu(gate[rows]) * _head_norm(o, gn)).astype(BF16)
    ret_ref[0] = state[...]

    mix = _dot(cat[...], wo_ref[...])
    y_ref[0] = _ln(DEEPNORM_ALPHA * x + mix, lng_ref[...], lnb_ref[...])


def _ab_prompt(x, w_in, w_pool, s_pool, gn_g, w_o, ln_g, ln_b):
    b, s, d = x.shape
    tm = min(512, s)
    cos, sin = _rope_tables(np.arange(s))
    tabs = _ret_tables(RET_CHUNK)
    nin = w_in.shape[1]
    in_specs = [
        pl.BlockSpec((1, tm, d), lambda bi, j: (bi, j, 0)),
        _full((d, nin)),
        pl.BlockSpec((tm, 128), lambda bi, j: (j, 0)),
        pl.BlockSpec((tm, 128), lambda bi, j: (j, 0)),
    ] + [_full(t.shape) for t in tabs] + [
        _full(w_pool.shape), _full(s_pool.shape), _full(gn_g.shape), _full(w_o.shape),
        _full(ln_g.shape), _full(ln_b.shape),
    ]
    out_shape = (jax.ShapeDtypeStruct((b, s, d), F32),
                 jax.ShapeDtypeStruct((b, POOL_HIST, POOL_WIDTH), F32),
                 jax.ShapeDtypeStruct((b, RET_QK, RET_V), F32))
    out_specs = (pl.BlockSpec((1, tm, d), lambda bi, j: (bi, j, 0)),
                 pl.BlockSpec((1, POOL_HIST, POOL_WIDTH), lambda bi, j: (bi, 0, 0)),
                 pl.BlockSpec((1, RET_QK, RET_V), lambda bi, j: (bi, 0, 0)))
    return pl.pallas_call(
        functools.partial(_ab_prompt_kernel, tm),
        out_shape=out_shape, grid=(b, s // tm), in_specs=in_specs, out_specs=out_specs,
        scratch_shapes=[pltpu.VMEM((POOL_HIST + tm, POOL_WIDTH), F32),
                        pltpu.VMEM((RET_QK, RET_V), F32),
                        pltpu.VMEM((tm, d), BF16)],
        compiler_params=_cparams("arbitrary", "arbitrary"), name="ab_prompt",
    )(x, w_in, jnp.asarray(cos), jnp.asarray(sin), *tabs, w_pool, s_pool, gn_g, w_o, ln_g, ln_b)


def _ab_sample_kernel(x_ref, win_ref, cos_ref, sin_ref, hmask_ref, gdec_ref, pool_ref, ret_ref,
                      wpool_ref, spool_ref, gn_ref, wo_ref, lng_ref, lnb_ref,
                      y_ref, npool_ref, nret_ref):
    x = x_ref[0]
    h = _dot(x.astype(BF16), win_ref[...])
    u = h[:, :POOL_WIDTH]
    prev = pool_ref[0]
    npool_ref[0, 0:POOL_STATE - 1, :] = prev[1:POOL_STATE, :]
    npool_ref[0, POOL_STATE - 1:POOL_STATE, :] = u[0:1, :]
    mixed = []
    for g, w in enumerate(POOL_WINDOWS):
        lanes = slice(g * POOL_GROUP_DIM, (g + 1) * POOL_GROUP_DIM)
        win = u[0:1, lanes] + jnp.sum(prev[POOL_STATE - (w - 1):, lanes], axis=0, keepdims=True)
        diff = win / float(w) - u[0:1, lanes]
        diff8 = jnp.broadcast_to(diff, (8, POOL_GROUP_DIM))
        mixed.append(_dot(diff8.astype(BF16), wpool_ref[g]) * spool_ref[:, lanes])
    a_out = jnp.concatenate(mixed, axis=1)

    cos = cos_ref[...]
    sin = sin_ref[...]
    qr = _rotary_cat(h[:, POOL_WIDTH:POOL_WIDTH + RET_QK], cos, sin)[0:1]
    kr = (_rotary_cat(h[:, POOL_WIDTH + RET_QK:POOL_WIDTH + 2 * RET_QK], cos, sin) * (RET_QK_DIM ** -0.5))[0:1]
    v = h[0:1, POOL_WIDTH + 2 * RET_QK:POOL_WIDTH + 2 * RET_QK + RET_V]
    gate = h[0:1, POOL_WIDTH + 2 * RET_QK + RET_V:]
    hm = hmask_ref[...]
    gdec = gdec_ref[...]
    sp = ret_ref[0]
    q4 = qr * hm
    k4 = kr * hm
    sc = jnp.sum(q4 * kr, axis=-1, keepdims=True)
    v4 = jnp.concatenate([v[:, hh * RET_V_DIM:(hh + 1) * RET_V_DIM] for hh in range(RET_HEADS)]
                         + [jnp.zeros((8 - RET_HEADS, RET_V_DIM), F32)], axis=0)
    o4 = sc.astype(BF16).astype(F32) * v4.astype(BF16).astype(F32) \
        + _dot((q4 * gdec).astype(BF16), sp.astype(BF16))
    kv = lax.dot_general(k4.astype(BF16), v4.astype(BF16), (((0,), (0,)), ((), ())), preferred_element_type=F32)
    gcol = jnp.sum(hm * gdec, axis=0, keepdims=True)
    nret_ref[0] = sp * gcol.T + kv
    mu = jnp.mean(o4, axis=-1, keepdims=True)
    oc = o4 - mu
    var = jnp.mean(oc * oc, axis=-1, keepdims=True)
    on = oc * lax.rsqrt(var + LN_EPS)
    on_row = jnp.concatenate([on[hh:hh + 1, :] for hh in range(RET_HEADS)], axis=1) * gn_ref[...]
    b_out = jnp.broadcast_to(_silu(gate) * on_row, (8, RET_V))
    cat = jnp.concatenate([a_out, b_out], axis=1).astype(BF16)
    mix = _dot(cat, wo_ref[...])
    y_ref[0] = _ln(DEEPNORM_ALPHA * x + mix, lng_ref[...], lnb_ref[...])


def _ab_sample(x8, w_in, pool_prev, ret_prev_perm, w_pool, s_pool, gn_g, w_o, ln_g, ln_b):
    b, _, d = x8.shape
    cos, sin = _rope_tables(np.array([PAST_LEN]))
    hmask = np.zeros((8, RET_QK), np.float32)
    hmask[:RET_HEADS] = (_lane_head()[None, :] == np.arange(RET_HEADS)[:, None])
    gdec = np.exp(_log_gamma())[_lane_head()][None, :].astype(np.float32)
    per_b = lambda shape: pl.BlockSpec((1,) + shape, lambda i: (i, 0, 0))
    in_specs = [per_b((8, d)), _full(w_in.shape), _full((1, 128)), _full((1, 128)), _full((8, RET_QK)),
                _full((1, RET_QK)), per_b((POOL_STATE, POOL_WIDTH)), per_b((RET_QK, RET_V_DIM)),
                _full(w_pool.shape), _full(s_pool.shape), _full(gn_g.shape), _full(w_o.shape),
                _full(ln_g.shape), _full(ln_b.shape)]
    out_shape = (jax.ShapeDtypeStruct((b, 8, d), F32),
                 jax.ShapeDtypeStruct((b, POOL_STATE, POOL_WIDTH), F32),
                 jax.ShapeDtypeStruct((b, RET_QK, RET_V_DIM), F32))
    out_specs = (per_b((8, d)), per_b((POOL_STATE, POOL_WIDTH)), per_b((RET_QK, RET_V_DIM)))
    return pl.pallas_call(
        _ab_sample_kernel, out_shape=out_shape, grid=(b,), in_specs=in_specs, out_specs=out_specs,
        compiler_params=_cparams("arbitrary"), name="ab_sample",
    )(x8, w_in, jnp.asarray(cos), jnp.asarray(sin), jnp.asarray(hmask), jnp.asarray(gdec),
      pool_prev, ret_prev_perm, w_pool, s_pool, gn_g, w_o, ln_g, ln_b)


def _mem_kv_kernel(m_ref, w_ref, k_ref, v_ref):
    h = _dot(m_ref[...].astype(BF16), w_ref[0])
    k_ref[0] = h[:, :D_MODEL]
    v_ref[0] = h[:, D_MODEL:]


def _mem_kv(mem2d, w_kv):
    n, d = mem2d.shape
    depth = w_kv.shape[0]
    out = jax.ShapeDtypeStruct((depth, n, d), F32)
    return pl.pallas_call(
        _mem_kv_kernel, out_shape=(out, out), grid=(depth,),
        in_specs=[_full((n, d)), pl.BlockSpec((1, d, 2 * d), lambda l: (l, 0, 0))],
        out_specs=(pl.BlockSpec((1, n, d), lambda l: (l, 0, 0)), pl.BlockSpec((1, n, d), lambda l: (l, 0, 0))),
        compiler_params=_cparams("arbitrary"), name="mem_kv",
    )(mem2d, w_kv)


def _mem_attn_kernel(x_ref, wq_ref, k_ref, v_ref, wo_ref, lng_ref, lnb_ref, y_ref):
    x = x_ref[0]
    q = _dot(x.astype(BF16), wq_ref[...]) * (MEM_HEAD_DIM ** -0.5)
    parts = []
    for h in range(MEM_HEADS):
        lanes = slice(h * MEM_HEAD_DIM, (h + 1) * MEM_HEAD_DIM)
        logits = _dot_nt(q[:, lanes].astype(BF16), k_ref[0, :, lanes].astype(BF16))
        m = jnp.max(logits, axis=-1, keepdims=True)
        p = jnp.exp(logits - m)
        l = jnp.sum(p, axis=-1, keepdims=True)
        parts.append(_dot(p.astype(BF16), v_ref[0, :, lanes].astype(BF16)) / l)
    o = jnp.concatenate(parts, axis=1)
    mix = _dot(o.astype(BF16), wo_ref[...])
    y_ref[0] = _ln(DEEPNORM_ALPHA * x + mix, lng_ref[...], lnb_ref[...])


def _mem_attn(x, w_q, k, v, w_o, ln_g, ln_b):
    b, s, d = x.shape
    tm = min(512, s)
    xs = pl.BlockSpec((1, tm, d), lambda bi, j: (bi, j, 0))
    kvs = pl.BlockSpec((1, N_MEM, d), lambda bi, j: (bi, 0, 0))
    return pl.pallas_call(
        _mem_attn_kernel, out_shape=jax.ShapeDtypeStruct((b, s, d), F32), grid=(b, s // tm),
        in_specs=[xs, _full(w_q.shape), kvs, kvs, _full(w_o.shape), _full(ln_g.shape), _full(ln_b.shape)],
        out_specs=xs, compiler_params=_cparams("arbitrary", "arbitrary"), name="mem_attn",
    )(x, w_q, k, v, w_o, ln_g, ln_b)


def _mem_attn_sample_kernel(x_ref, wq_ref, k_ref, v_ref, wo_ref, lng_ref, lnb_ref, y_ref, q_s, o_s):
    b = pl.program_id(0)
    nb = pl.num_programs(0)

    @pl.when(b == 0)
    def _():
        q_s[...] = _dot(x_ref[...].astype(BF16), wq_ref[...]) * (MEM_HEAD_DIM ** -0.5)

    qrow = q_s[pl.ds(b, 1), :]
    q4 = jnp.concatenate([qrow[:, h * MEM_HEAD_DIM:(h + 1) * MEM_HEAD_DIM] for h in range(MEM_HEADS)]
                         + [jnp.zeros((8 - MEM_HEADS, MEM_HEAD_DIM), F32)], axis=0)
    k2 = k_ref[0, 0].reshape(N_MEM * MEM_HEADS, MEM_HEAD_DIM).astype(BF16)
    v2 = v_ref[0, 0].reshape(N_MEM * MEM_HEADS, MEM_HEAD_DIM).astype(BF16)
    logits = _dot_nt(q4.astype(BF16), k2)
    col = lax.broadcasted_iota(jnp.int32, logits.shape, 1)
    rowi = lax.broadcasted_iota(jnp.int32, logits.shape, 0)
    own = (col % MEM_HEADS) == rowi
    m = jnp.max(jnp.where(own, logits, -jnp.inf), axis=-1, keepdims=True)
    m = jnp.where(rowi[:, :1] < MEM_HEADS, m, 0.0)
    p = jnp.where(own, jnp.exp(logits - m), 0.0)
    l = jnp.maximum(jnp.sum(p, axis=-1, keepdims=True), jnp.where(rowi[:, :1] < MEM_HEADS, 0.0, 1.0))
    o4 = _dot(p.astype(BF16), v2) / l
    o_s[pl.ds(b, 1), :] = jnp.concatenate([o4[h:h + 1, :] for h in range(MEM_HEADS)], axis=1)

    @pl.when(b == nb - 1)
    def _():
        mix = _dot(o_s[...].astype(BF16), wo_ref[...])
        y_ref[...] = _ln(DEEPNORM_ALPHA * x_ref[...] + mix, lng_ref[...], lnb_ref[...])


def _mem_attn_sample(x2d, w_q, cache_k, cache_v, layer, w_o, ln_g, ln_b):
    n, d = x2d.shape
    kvs = pl.BlockSpec((1, 1, N_MEM, MEM_HEADS, MEM_HEAD_DIM), lambda b: (layer, b, 0, 0, 0))
    return pl.pallas_call(
        _mem_attn_sample_kernel, out_shape=jax.ShapeDtypeStruct((n, d), F32), grid=(n,),
        in_specs=[_full((n, d)), _full(w_q.shape), kvs, kvs, _full(w_o.shape), _full(ln_g.shape), _full(ln_b.shape)],
        out_specs=_full((n, d)),
        scratch_shapes=[pltpu.VMEM((n, d), F32), pltpu.VMEM((n, d), F32)],
        compiler_params=_cparams("arbitrary"), name="mem_attn_sample",
    )(x2d, w_q, cache_k, cache_v, w_o, ln_g, ln_b)


def _router_gates(x, wr_hi, wr_lo, br):
    xh, xl = _split_bf16(x)
    lg = _dot(xh, wr_hi) + (_dot(xh, wr_lo) + _dot(xl, wr_hi)) + br
    n = x.shape[0]
    lane_i = lax.broadcasted_iota(jnp.int32, (n, ROUTER_LANES), 1)
    lane = lane_i.astype(F32)
    is_c = lane_i < N_GROUPS
    neg = jnp.float32(-jnp.inf)
    big = jnp.float32(ROUTER_LANES)
    coarse = jnp.where(is_c, lg, neg)
    cmax = jnp.max(coarse, axis=-1, keepdims=True)
    grp = jnp.min(jnp.where(coarse == cmax, lane, big), axis=-1, keepdims=True)
    p_grp = 1.0 / jnp.sum(jnp.where(is_c, jnp.exp(lg - cmax), 0.0), axis=-1, keepdims=True)
    lane_grp = ((lane_i - N_GROUPS) // EXPERTS_PER_GROUP).astype(F32)
    in_grp = (lane_i >= N_GROUPS) & (lane_i < N_GROUPS + N_EXPERTS) & (lane_grp == grp)
    f1 = jnp.where(in_grp, lg, neg)
    t1 = jnp.max(f1, axis=-1, keepdims=True)
    i1 = jnp.min(jnp.where(f1 == t1, lane, big), axis=-1, keepdims=True)
    f2 = jnp.where(lane == i1, neg, f1)
    t2 = jnp.max(f2, axis=-1, keepdims=True)
    i2 = jnp.min(jnp.where(f2 == t2, lane, big), axis=-1, keepdims=True)
    e2 = jnp.exp(t2 - t1)
    w1 = p_grp / (1.0 + e2)
    w2 = p_grp * e2 / (1.0 + e2)
    return jnp.where(lane == i1, w1, 0.0) + jnp.where(lane == i2, w2, 0.0) + jnp.where(lane == grp, 1.0, 0.0)


MOE_TILE = 1024
MOE_SUB = 512
MOE_CHUNK = 160


def _moe_sparse_kernel(x_ref, wrh_ref, wrl_ref, br_ref, tri_ref, wg_ref, wu_ref, wd_ref, lng_ref, lnb_ref, y_ref,
                       xb, gates3, onehot, rank, cnt, acc):
    g = pl.program_id(1)
    tm = x_ref.shape[0]
    nsub = tm // MOE_SUB
    lane = lax.broadcasted_iota(jnp.int32, (MOE_SUB, ROUTER_LANES), 1)

    @pl.when(g == 0)
    def _():
        x = x_ref[...]
        xb[...] = x.astype(BF16)
        gt = _router_gates(x, wrh_ref[...], wrl_ref[...], br_ref[...])
        hi, mid, lo = _split3(gt)
        gates3[0] = hi
        gates3[1] = mid
        gates3[2] = lo
        lane_t = lax.broadcasted_iota(jnp.int32, gt.shape, 1)
        oh = jnp.where(lane_t < N_GROUPS, gt, 0.0).astype(BF16)
        onehot[...] = oh
        cnt[...] = jnp.zeros(cnt.shape, F32)
        for st in range(nsub):
            rows = slice(st * MOE_SUB, (st + 1) * MOE_SUB)
            pref = _dot(tri_ref[...], oh[rows])
            rank[rows, :] = pref - 1.0
            cnt[st:st + 1, :] = pref[MOE_SUB - 1:, :]
        acc[...] = jnp.zeros(acc.shape, F32)

    lane_c = lax.broadcasted_iota(jnp.int32, (MOE_SUB, MOE_CHUNK), 1).astype(F32)
    lane_cnt = lax.broadcasted_iota(jnp.int32, cnt.shape, 1)
    most = jnp.max(jnp.where(lane_cnt == g, cnt[...], 0.0))
    for c in range(-(-MOE_SUB // MOE_CHUNK)):
        @pl.when(most > float(c * MOE_CHUNK))
        def _(c=c):
            sels, xs, gs = [], [], []
            for st in range(nsub):
                rows = slice(st * MOE_SUB, (st + 1) * MOE_SUB)
                member = jnp.sum(jnp.where(lane == g, onehot[rows, :].astype(F32), 0.0), axis=-1, keepdims=True)
                rk = jnp.sum(jnp.where(lane == g, rank[rows, :], 0.0), axis=-1, keepdims=True) - float(c * MOE_CHUNK)
                sel = jnp.where((member > 0.0) & (rk == lane_c), 1.0, 0.0).astype(BF16)
                sels.append(sel)
                tn = lambda a, sel=sel: lax.dot_general(sel, a, (((0,), (0,)), ((), ())), preferred_element_type=F32)
                xs.append(tn(xb[rows, :]).astype(BF16))
                gs.append(tn(gates3[0, rows, :]) + (tn(gates3[1, rows, :]) + tn(gates3[2, rows, :])))
            xg = jnp.concatenate(xs, axis=0)
            gg = jnp.concatenate(gs, axis=0)
            lane_g = lax.broadcasted_iota(jnp.int32, gg.shape, 1)
            y = jnp.zeros((nsub * MOE_CHUNK, x_ref.shape[1]), F32)
            for e in range(EXPERTS_PER_GROUP):
                gcol = jnp.sum(jnp.where(lane_g == N_GROUPS + EXPERTS_PER_GROUP * g + e, gg, 0.0),
                               axis=-1, keepdims=True)
                hid = _silu(_dot(xg, wg_ref[0, e])) * _dot(xg, wu_ref[0, e]) * gcol
                y = y + _dot(hid.astype(BF16), wd_ref[0, e])
            yh, yl = _split_bf16(y)
            for st in range(nsub):
                rows = slice(st * MOE_SUB, (st + 1) * MOE_SUB)
                part = slice(st * MOE_CHUNK, (st + 1) * MOE_CHUNK)
                acc[rows, :] += _dot(sels[st], yh[part]) + _dot(sels[st], yl[part])

    @pl.when(g == N_GROUPS - 1)
    def _():
        y_ref[...] = _ln(DEEPNORM_ALPHA * x_ref[...] + acc[...], lng_ref[...], lnb_ref[...])


def _moe_sparse(x2d, wr_hi, wr_lo, br, w_gate, w_up, w_down, ln_g, ln_b):
    n, d = x2d.shape
    tm = MOE_TILE
    assert n % tm == 0 and tm % MOE_SUB == 0
    tri = jnp.asarray(np.tril(np.ones((MOE_SUB, MOE_SUB), np.float32)), dtype=BF16)
    grouped = lambda w: w.reshape(N_GROUPS, EXPERTS_PER_GROUP, w.shape[1], w.shape[2])
    xs = pl.BlockSpec((tm, d), lambda i, g: (i, 0))
    wspec = lambda a, b: pl.BlockSpec((1, EXPERTS_PER_GROUP, a, b), lambda i, g: (g, 0, 0, 0))
    return pl.pallas_call(
        _moe_sparse_kernel, out_shape=jax.ShapeDtypeStruct((n, d), F32), grid=(n // tm, N_GROUPS),
        in_specs=[xs, _full(wr_hi.shape), _full(wr_lo.shape), _full(br.shape), _full(tri.shape),
                  wspec(d, EXPERT_HIDDEN), wspec(d, EXPERT_HIDDEN), wspec(EXPERT_HIDDEN, d),
                  _full(ln_g.shape), _full(ln_b.shape)],
        out_specs=xs,
        scratch_shapes=[pltpu.VMEM((tm, d), BF16), pltpu.VMEM((3, tm, ROUTER_LANES), BF16),
                        pltpu.VMEM((tm, ROUTER_LANES), BF16), pltpu.VMEM((tm, ROUTER_LANES), F32),
                        pltpu.VMEM((8, ROUTER_LANES), F32), pltpu.VMEM((tm, d), F32)],
        compiler_params=_cparams("arbitrary", "arbitrary"), name="moe_sparse",
    )(x2d, wr_hi, wr_lo, br, tri, grouped(w_gate), grouped(w_up), grouped(w_down), ln_g, ln_b)


def _moe_dense_kernel(x_ref, wrh_ref, wrl_ref, br_ref, wg_ref, wu_ref, wd_ref, lng_ref, lnb_ref, y_ref,
                      xb, gates, acc):
    e = pl.program_id(1)

    @pl.when(e == 0)
    def _():
        x = x_ref[...]
        xb[...] = x.astype(BF16)
        gates[...] = _router_gates(x, wrh_ref[...], wrl_ref[...], br_ref[...])
        acc[...] = jnp.zeros(acc.shape, F32)

    lane_i = lax.broadcasted_iota(jnp.int32, gates.shape, 1)
    gcol = jnp.sum(jnp.where(lane_i == e + N_GROUPS, gates[...], 0.0), axis=-1, keepdims=True)
    xv = xb[...]
    hid = _silu(_dot(xv, wg_ref[0])) * _dot(xv, wu_ref[0]) * gcol
    acc[...] += _dot(hid.astype(BF16), wd_ref[0])

    @pl.when(e == N_EXPERTS - 1)
    def _():
        y_ref[...] = _ln(DEEPNORM_ALPHA * x_ref[...] + acc[...], lng_ref[...], lnb_ref[...])


def _moe_dense(x2d, wr_hi, wr_lo, br, w_gate, w_up, w_down, ln_g, ln_b):
    n, d = x2d.shape
    tm = min(1024, n)
    xs = pl.BlockSpec((tm, d), lambda i, e: (i, 0))
    return pl.pallas_call(
        _moe_dense_kernel, out_shape=jax.ShapeDtypeStruct((n, d), F32), grid=(n // tm, N_EXPERTS),
        in_specs=[xs, _full(wr_hi.shape), _full(wr_lo.shape), _full(br.shape),
                  pl.BlockSpec((1, d, EXPERT_HIDDEN), lambda i, e: (e, 0, 0)),
                  pl.BlockSpec((1, d, EXPERT_HIDDEN), lambda i, e: (e, 0, 0)),
                  pl.BlockSpec((1, EXPERT_HIDDEN, d), lambda i, e: (e, 0, 0)),
                  _full(ln_g.shape), _full(ln_b.shape)],
        out_specs=xs,
        scratch_shapes=[pltpu.VMEM((tm, d), BF16), pltpu.VMEM((tm, ROUTER_LANES), F32), pltpu.VMEM((tm, d), F32)],
        compiler_params=_cparams("arbitrary", "arbitrary"), name="moe_dense",
    )(x2d, wr_hi, wr_lo, br, w_gate, w_up, w_down, ln_g, ln_b)


def _log_sigmoid(z):
    return jnp.minimum(z, 0.0) - jnp.log1p(jnp.exp(-jnp.abs(z)))


FOX_AUG = 2 * FOX_HEADS * FOX_HEAD_DIM
FOX_TILE = 512


def _fox_aug_tables():
    pq = np.zeros((128, FOX_AUG), np.float32)
    pk = np.zeros((128, FOX_AUG), np.float32)
    cq = np.zeros((1, FOX_AUG), np.float32)
    ck = np.zeros((1, FOX_AUG), np.float32)
    cv = np.zeros((1, FOX_AUG), np.float32)
    dm = np.zeros((1, FOX_AUG), np.float32)
    for h in range(FOX_HEADS):
        base = 128 * h
        off = base + (FOX_HEAD_DIM if h % 2 == 0 else 0)
        data = base + (0 if h % 2 == 0 else FOX_HEAD_DIM)
        dm[0, data:data + FOX_HEAD_DIM] = 1.0
        for g in range(3):
            pq[16 * g + h, off + g] = 1.0
            cq[0, off + 3 + g] = 1.0
            ck[0, off + g] = 1.0
            pk[16 * g + h, off + 3 + g] = -1.0
        cv[0, off] = 1.0
    return pq, pk, cq, ck, cv.T.copy(), dm, dm.T.copy()


def _fox_proj_kernel(x_ref, w_ref, wvt_ref, wf_ref, bf_ref, tri_ref, pq_ref, pk_ref, cq_ref, ck_ref, cvt_ref,
                     dm_ref, dmt_ref, qa_ref, ka_ref, vat_ref, k_ref, v_ref, lf_ref, carry):
    j = pl.program_id(1)

    @pl.when(j == 0)
    def _():
        carry[...] = jnp.zeros(carry.shape, F32)

    xb = x_ref[0].astype(BF16)
    h = _dot(xb, w_ref[...])
    q2 = h[:, :D_MODEL] * (FOX_HEAD_DIM ** -0.5 * LOG2E)
    k = h[:, D_MODEL:2 * D_MODEL]
    k_ref[0] = k
    v_ref[0] = h[:, 2 * D_MODEL:]
    vt = _dot_nt(wvt_ref[...], xb)
    lf = _log_sigmoid(_dot(xb, wf_ref[...]) + bf_ref[...])
    lf_ref[0] = lf[:, :FOX_HEADS]
    hi, mid, lo = _split3(lf)
    tri = tri_ref[...]
    c = _dot(tri, hi) + (_dot(tri, mid) + _dot(tri, lo)) + carry[...]
    carry[...] = c[c.shape[0] - 1:, :]
    chi, cmid, clo = _split3(c * LOG2E)
    lane = lax.broadcasted_iota(jnp.int32, c.shape, 1)
    c3 = jnp.where(lane < 16, chi.astype(F32), jnp.where(lane < 32, cmid.astype(F32), clo.astype(F32))).astype(BF16)
    qb = _dot(c3, pq_ref[...]) + cq_ref[...]
    kb = _dot(c3, pk_ref[...]) + ck_ref[...]
    for hh in range(FOX_HEADS):
        blk = slice(128 * hh, 128 * hh + 128)
        src = slice(128 * (hh // 2), 128 * (hh // 2) + 128)
        is_data = dm_ref[:, blk] > 0.0
        qa_ref[0, :, blk] = jnp.where(is_data, q2[:, src], qb[:, blk]).astype(BF16)
        ka_ref[0, :, blk] = jnp.where(is_data, k[:, src], kb[:, blk]).astype(BF16)
        vat_ref[0, blk, :] = jnp.where(dmt_ref[blk, :] > 0.0, vt[src, :], cvt_ref[blk, :]).astype(BF16)


def _fox_proj(x, w_qkv, w_vt, w_f3, b_f3):
    b, s, d = x.shape
    tm = min(FOX_TILE, s)
    tri = jnp.asarray(np.tril(np.ones((tm, tm), np.float32)), dtype=BF16)
    pq, pk, cq, ck, cvt, dm, dmt = _fox_aug_tables()
    consts = [jnp.asarray(pq, dtype=BF16), jnp.asarray(pk, dtype=BF16), jnp.asarray(cq), jnp.asarray(ck),
              jnp.asarray(cvt), jnp.asarray(dm), jnp.asarray(dmt)]
    xs = pl.BlockSpec((1, tm, d), lambda bi, j: (bi, j, 0))
    augs = pl.BlockSpec((1, tm, FOX_AUG), lambda bi, j: (bi, j, 0))
    f32o = jax.ShapeDtypeStruct((b, s, d), F32)
    augo = jax.ShapeDtypeStruct((b, s, FOX_AUG), BF16)
    return pl.pallas_call(
        _fox_proj_kernel,
        out_shape=(augo, augo, jax.ShapeDtypeStruct((b, FOX_AUG, s), BF16), f32o, f32o,
                   jax.ShapeDtypeStruct((b, s, FOX_HEADS), F32)),
        grid=(b, s // tm),
        in_specs=[xs, _full(w_qkv.shape), _full(w_vt.shape), _full(w_f3.shape), _full(b_f3.shape), _full(tri.shape)]
        + [_full(c.shape) for c in consts],
        out_specs=(augs, augs, pl.BlockSpec((1, FOX_AUG, tm), lambda bi, j: (bi, 0, j)), xs, xs,
                   pl.BlockSpec((1, tm, FOX_HEADS), lambda bi, j: (bi, j, 0))),
        scratch_shapes=[pltpu.VMEM((1, 128), F32)],
        compiler_params=_cparams("arbitrary", "arbitrary"), name="fox_proj",
    )(x, w_qkv, w_vt, w_f3, b_f3, tri, *consts)


def _fox_attn_body(tq, i, q_ref, k_ref, vt_ref, o_ref, m_s, acc_s, s_buf):
    krow = lax.broadcasted_iota(jnp.int32, (tq, tq), 0)
    qcol = lax.broadcasted_iota(jnp.int32, (tq, tq), 1)
    m_s[...] = jnp.full(m_s.shape, -jnp.inf, F32)
    acc_s[...] = jnp.zeros(acc_s.shape, F32)
    heads = [slice(128 * hd, 128 * hd + 128) for hd in range(2)]

    def scores(jj, diagonal):
        k0 = pl.multiple_of(jj * tq, tq)
        sts = [_dot_nt(k_ref[0, pl.ds(k0, tq), lanes], q_ref[0, :, lanes]) for lanes in heads]
        if diagonal:
            sts = [jnp.where(krow <= qcol, st, -jnp.inf) for st in sts]
        return sts

    def consume(jj):
        k0 = pl.multiple_of(jj * tq, tq)
        for hd, lanes in enumerate(heads):
            st = s_buf[hd]
            m_old = m_s[hd]
            m_new = jnp.maximum(m_old, jnp.max(st, axis=0, keepdims=True))
            pt = jnp.exp2(st - m_new).astype(BF16)
            acc_s[hd] = jnp.exp2(m_old - m_new) * acc_s[hd] + _dot(vt_ref[0, lanes, pl.ds(k0, tq)], pt)
            m_s[hd] = m_new

    def step(jj, diagonal_next):
        nxt = scores(jj + 1, diagonal_next)
        consume(jj)
        for hd in range(2):
            s_buf[hd] = nxt[hd]

    @pl.when(i == 0)
    def _():
        for hd, st in enumerate(scores(0, True)):
            s_buf[hd] = st

    @pl.when(i > 0)
    def _():
        for hd, st in enumerate(scores(0, False)):
            s_buf[hd] = st

        def body(jj, carry):
            step(jj, False)
            return carry

        lax.fori_loop(0, i - 1, body, 0)
        step(i - 1, True)

    consume(i)
    outs = []
    for hd in range(2):
        acc = acc_s[hd]
        l_row = FOX_HEAD_DIM if hd == 0 else 0
        outs.append(acc / acc[l_row:l_row + 1, :])
    row = lax.broadcasted_iota(jnp.int32, (128, tq), 0)
    o_ref[0] = jnp.where(row < FOX_HEAD_DIM, outs[0], outs[1]).T.astype(BF16)


def _proj_ln_kernel(o_ref, x_ref, w_ref, lng_ref, lnb_ref, y_ref):
    mix = _dot(o_ref[...].astype(BF16), w_ref[...])
    y_ref[...] = _ln(DEEPNORM_ALPHA * x_ref[...] + mix, lng_ref[...], lnb_ref[...])


def _proj_ln(o2d, x2d, w, ln_g, ln_b):
    n, d = x2d.shape
    tm = min(1024, n)
    xs = pl.BlockSpec((tm, d), lambda i: (i, 0))
    return pl.pallas_call(
        _proj_ln_kernel, out_shape=jax.ShapeDtypeStruct((n, d), F32), grid=(n // tm,),
        in_specs=[xs, xs, _full(w.shape), _full(ln_g.shape), _full(ln_b.shape)], out_specs=xs,
        compiler_params=_cparams("arbitrary"), name="proj_ln",
    )(o2d, x2d, w, ln_g, ln_b)


def _fox_proj_sample_kernel(x_ref, w_ref, wf_ref, bf_ref, q_ref, k_ref, v_ref, lf_ref):
    xb = x_ref[...].astype(BF16)
    h = _dot(xb, w_ref[...])
    q_ref[...] = h[:, :D_MODEL] * (FOX_HEAD_DIM ** -0.5)
    k_ref[...] = h[:, D_MODEL:2 * D_MODEL]
    v_ref[...] = h[:, 2 * D_MODEL:]
    lf_ref[...] = _log_sigmoid(_dot(xb, wf_ref[...]) + bf_ref[...])


def _fox_proj_sample(x2d, w_qkv, w_f, b_f):
    n, d = x2d.shape
    o = jax.ShapeDtypeStruct((n, d), F32)
    return pl.pallas_call(
        _fox_proj_sample_kernel, out_shape=(o, o, o, jax.ShapeDtypeStruct((n, 128), F32)),
        name="fox_proj_sample", compiler_params=pltpu.CompilerParams(vmem_limit_bytes=VMEM_LIMIT),
    )(x2d, w_qkv, w_f, b_f)


PAGES_PER_STEP = 8


def _fox_paged_body(s, ns, q_ref, kn_ref, vn_ref, lfn_ref, hm_ref, ut_ref, k_refs, v_refs, lf_refs, o_ref,
                    qb, m_s, l_s, acc_s, ccar):
    npp = len(k_refs)
    nh, hd = FOX_HEADS, FOX_HEAD_DIM

    @pl.when(s == 0)
    def _():
        qb[...] = jnp.broadcast_to(q_ref[0], (128, D_MODEL)).T
        m_s[...] = jnp.full(m_s.shape, -jnp.inf, F32)
        l_s[...] = jnp.zeros(l_s.shape, F32)
        acc_s[...] = jnp.zeros(acc_s.shape, F32)
        ccar[...] = jnp.zeros(ccar.shape, F32)

    lf = jnp.concatenate([r[0] for r in lf_refs], axis=0)
    hi, mid, lo = _split3(lf)
    ut = ut_ref[...]
    cs_tot = _dot(hi, ut) + (_dot(mid, ut) + _dot(lo, ut))
    base = ccar[...]
    svals = []
    for p in range(npp):
        rows = slice(p * nh, (p + 1) * nh)
        c_p = cs_tot[rows, :128] + base
        base = base + cs_tot[rows, 128:]
        prod = k_refs[p][0] * qb[...]
        svals.append(jnp.sum(prod.reshape(nh, hd, 128), axis=1) - c_p)
    ccar[...] = base

    smax = svals[0]
    for p in range(1, npp):
        smax = jnp.maximum(smax, svals[p])
    m_old = m_s[...]
    m_new = jnp.maximum(m_old, jnp.max(smax, axis=-1, keepdims=True))
    alpha = jnp.exp(m_old - m_new)
    ps = [jnp.exp(sv - m_new) for sv in svals]
    psum = ps[0]
    for p in range(1, npp):
        psum = psum + ps[p]
    l_s[...] = alpha * l_s[...] + jnp.sum(psum, axis=-1, keepdims=True)
    m_s[...] = m_new

    def rows64(x):
        return jnp.concatenate([jnp.broadcast_to(x[h:h + 1, :], (hd, 128)) for h in range(nh)], axis=0)

    contrib = rows64(ps[0]) * v_refs[0][0]
    for p in range(1, npp):
        contrib = contrib + rows64(ps[p]) * v_refs[p][0]
    acc_s[...] = acc_s[...] * rows64(alpha) + contrib

    @pl.when(s == ns - 1)
    def _():
        hm = hm_ref[...]
        lane = lax.broadcasted_iota(jnp.int32, (nh, 128), 1)
        head = lax.broadcasted_iota(jnp.int32, (nh, 128), 0)
        lf_col = jnp.sum(jnp.where(lane == head, jnp.broadcast_to(lfn_ref[0], (nh, 128)), 0.0),
                         axis=-1, keepdims=True)
        qk = jnp.broadcast_to(q_ref[0] * kn_ref[0], (nh, D_MODEL)) * hm
        s_new = jnp.sum(qk, axis=-1, keepdims=True) - (ccar[...] + lf_col)
        m_prev = m_s[...]
        m_fin = jnp.maximum(m_prev, s_new)
        a2 = jnp.exp(m_prev - m_fin)
        pn = jnp.exp(s_new - m_fin)
        l_fin = a2 * l_s[...] + pn
        vnb = jnp.broadcast_to(vn_ref[0], (128, D_MODEL)).T
        lane_w = lax.broadcasted_iota(jnp.int32, (D_MODEL, 128), 1)
        tot = acc_s[...] * rows64(a2) + jnp.where(lane_w == 0, rows64(pn) * vnb, 0.0)
        o_col = jnp.sum(tot, axis=-1, keepdims=True) / rows64(l_fin)
        o_ref[0] = o_col.T[0:1, :]


def _fox_attn_paged_kernel(tq, npp, pt_ref, qa_ref, ka_ref, vat_ref, q_ref, kn_ref, vn_ref, lfn_ref, hm_ref, ut_ref,
                           *rest):
    del pt_ref
    k_refs, v_refs, lf_refs = rest[:npp], rest[npp:2 * npp], rest[2 * npp:3 * npp]
    o_att, o_smp = rest[3 * npp:3 * npp + 2]
    m_a, acc_a, s_buf, qb, m_p, l_p, acc_p, ccar = rest[3 * npp + 2:]
    i = pl.program_id(2)
    _fox_attn_body(tq, i, qa_ref, ka_ref, vat_ref, o_att, m_a, acc_a, s_buf)
    _fox_paged_body(i, pl.num_programs(2), q_ref, kn_ref, vn_ref, lfn_ref, hm_ref, ut_ref, k_refs, v_refs, lf_refs,
                    o_smp, qb, m_p, l_p, acc_p, ccar)


def _fox_attn_paged(qa, ka, vat, q, k_new, v_new, lf_new, cache_kt, cache_vt, cache_lft, page_table):
    bp, s, _ = qa.shape
    bs, d = q.shape
    tq = min(FOX_TILE, s)
    nq = s // tq
    n_pages = page_table.shape[1]
    npp = PAGES_PER_STEP
    hm = np.zeros((FOX_HEADS, d), np.float32)
    for h in range(FOX_HEADS):
        hm[h, h * FOX_HEAD_DIM:(h + 1) * FOX_HEAD_DIM] = 1.0
    ut = np.concatenate([np.triu(np.ones((128, 128), np.float32)), np.ones((128, 128), np.float32)], axis=1)
    assert bs == bp * (FOX_HEADS // 2) and n_pages == npp * nq, (bs, bp, n_pages, nq)
    seq = lambda bi, hp: bi * (FOX_HEADS // 2) + hp
    row = lambda w: pl.BlockSpec((1, 1, w), lambda bi, hp, i, pt: (seq(bi, hp), 0, 0))
    const = lambda shape: pl.BlockSpec(shape, lambda bi, hp, i, pt: (0,) * len(shape))

    def page_spec(p, r):
        return pl.BlockSpec((1, r, PAGE_SIZE),
                            lambda bi, hp, i, pt: (pt[seq(bi, hp) * n_pages + i * npp + p], 0, 0))

    qs = pl.BlockSpec((1, tq, 256), lambda bi, hp, i, pt: (bi, i, hp))
    in_specs = ([qs, pl.BlockSpec((1, s, 256), lambda bi, hp, i, pt: (bi, 0, hp)),
                 pl.BlockSpec((1, 256, s), lambda bi, hp, i, pt: (bi, hp, 0)),
                 row(d), row(d), row(d), row(128), const((FOX_HEADS, d)), const((128, 256))]
                + [page_spec(p, d) for p in range(npp)] + [page_spec(p, d) for p in range(npp)]
                + [page_spec(p, FOX_HEADS) for p in range(npp)])
    grid_spec = pltpu.PrefetchScalarGridSpec(
        num_scalar_prefetch=1, grid=(bp, FOX_HEADS // 2, nq), in_specs=in_specs,
        out_specs=(pl.BlockSpec((1, tq, 128), lambda bi, hp, i, pt: (bi, i, hp)), row(d)),
        scratch_shapes=[pltpu.VMEM((2, 1, tq), F32), pltpu.VMEM((2, 128, tq), F32), pltpu.VMEM((2, tq, tq), F32),
                        pltpu.VMEM((d, 128), F32), pltpu.VMEM((FOX_HEADS, 128), F32),
                        pltpu.VMEM((FOX_HEADS, 128), F32), pltpu.VMEM((d, 128), F32),
                        pltpu.VMEM((FOX_HEADS, 128), F32)])
    att, out = pl.pallas_call(
        functools.partial(_fox_attn_paged_kernel, tq, npp),
        out_shape=(jax.ShapeDtypeStruct((bp, s, D_MODEL), BF16), jax.ShapeDtypeStruct((bs, 1, d), F32)),
        grid_spec=grid_spec, compiler_params=_cparams("arbitrary", "arbitrary", "arbitrary"), name="fox_attn_paged",
    )(page_table.reshape(-1), qa, ka, vat, q[:, None, :], k_new[:, None, :], v_new[:, None, :], lf_new[:, None, :],
      jnp.asarray(hm), jnp.asarray(ut, dtype=BF16),
      *([cache_kt] * npp), *([cache_vt] * npp), *([cache_lft] * npp))
    return att, out[:, 0, :]


def _router_weights(w_gc, b_gc, w_gf, b_gf):
    d = w_gc.shape[0]
    wf = jnp.transpose(w_gf, (1, 0, 2)).reshape(d, N_EXPERTS)
    wr = jnp.zeros((d, ROUTER_LANES), F32).at[:, :N_GROUPS].set(w_gc).at[:, N_GROUPS:N_GROUPS + N_EXPERTS].set(wf)
    br = jnp.zeros((1, ROUTER_LANES), F32).at[0, :N_GROUPS].set(b_gc).at[0, N_GROUPS:N_GROUPS + N_EXPERTS].set(
        b_gf.reshape(-1))
    hi = wr.astype(BF16)
    lo = (wr - hi.astype(F32)).astype(BF16)
    return hi, lo, br


def kernel(x_prompt, x_sample, mem_prompt, state_pool, state_ret, cache_fox_k, cache_fox_v, cache_fox_logf,
           cache_mem_k, cache_mem_v, page_table, ln_g, ln_b, ab_w_in, pool_w, pool_scale, ret_gn_g, ab_w_o,
           fox_w_in, fox_b_f, fox_w_o, mem_wq, mem_wkv, mem_wo, moe_w_gc, moe_b_gc, moe_w_gf, moe_b_gf,
           moe_w_up, moe_w_gate, moe_w_down):
    bp, s, d = x_prompt.shape
    bs = x_sample.shape[0]
    n_phys = cache_fox_k.shape[1]
    row = lambda a: a.reshape(1, -1)
    rows8 = lambda y2: jnp.broadcast_to(y2[:, None, :], (bs, 8, d))

    yp = x_prompt
    ys = x_sample.reshape(bs, d)
    mk_all, mv_all = _mem_kv(mem_prompt.reshape(bp * N_MEM, d), mem_wkv.astype(BF16))

    perm = _qk_perm()
    inv_perm = np.argsort(perm)
    for l in range(DEPTH):
        g0, b0 = row(ln_g[l, 0]), row(ln_b[l, 0])
        if l % 2 == 0:
            e = l // 2
            cols = np.concatenate([np.arange(POOL_WIDTH), POOL_WIDTH + perm, POOL_WIDTH + RET_QK + perm,
                                   np.arange(POOL_WIDTH + 2 * RET_QK, ab_w_in.shape[-1])])
            w_in = ab_w_in[e][:, cols].astype(BF16)
            w_pool = pool_w[e].astype(BF16)
            w_o = ab_w_o[e].astype(BF16)
            yp, pool_p, ret_p = _ab_prompt(yp, w_in, w_pool, row(pool_scale[e]), row(ret_gn_g[e]), w_o, g0, b0)
            ret_prev = state_ret[e].reshape(bs, RET_QK, RET_V_DIM)[:, perm, :]
            y8, pool_s, ret_s = _ab_sample(rows8(ys), w_in, state_pool[e], ret_prev, w_pool, row(pool_scale[e]),
                                           row(ret_gn_g[e]), w_o, g0, b0)
            ys = y8[:, 0, :]
            new_pool_p = pool_p[:, POOL_HIST - POOL_STATE:, :][None]
            new_pool_s = pool_s[None]
            rp = ret_p[:, inv_perm, :].reshape(bp, RET_HEADS, RET_QK_DIM, RET_HEADS, RET_V_DIM)
            new_ret_p = jnp.stack([rp[:, h, :, h, :] for h in range(RET_HEADS)], axis=1)[None]
            new_ret_s = ret_s[:, inv_perm, :].reshape(bs, RET_HEADS, RET_QK_DIM, RET_V_DIM)[None]
        else:
            o = l // 2
            w = fox_w_in[o]
            w_qkv = w[:, :3 * d].astype(BF16)
            wf = w[:, 3 * d:]
            w_f3 = jnp.pad(jnp.concatenate([wf, wf, wf], axis=1), ((0, 0), (0, 128 - 3 * FOX_HEADS))).astype(BF16)
            bf = fox_b_f[o]
            b_f3 = jnp.pad(jnp.concatenate([bf, bf, bf]), (0, 128 - 3 * FOX_HEADS)).reshape(1, 128)
            w_o = fox_w_o[o].astype(BF16)
            w_vt = jnp.transpose(w[:, 2 * d:3 * d]).astype(BF16)
            qa, ka, vat, k, v, lf = _fox_proj(yp, w_qkv, w_vt, w_f3, b_f3)
            qs, ks, vs, lfs = _fox_proj_sample(ys, w_qkv, w_f3, b_f3)
            kt = jnp.transpose(cache_fox_k[o], (0, 2, 3, 1)).reshape(n_phys, d, PAGE_SIZE)
            vt = jnp.transpose(cache_fox_v[o], (0, 2, 3, 1)).reshape(n_phys, d, PAGE_SIZE)
            lft = jnp.transpose(cache_fox_logf[o], (0, 2, 1))
            att, att_s = _fox_attn_paged(qa, ka, vat, qs, ks, vs, lfs, kt, vt, lft, page_table)
            yp = _proj_ln(att.reshape(bp * s, d), yp.reshape(bp * s, d), w_o, g0, b0).reshape(bp, s, d)
            new_fk_p = k.reshape(1, bp, s, FOX_HEADS, FOX_HEAD_DIM)
            new_fv_p = v.reshape(1, bp, s, FOX_HEADS, FOX_HEAD_DIM)
            new_fl_p = lf[None]
            ys = _proj_ln(att_s, ys, w_o, g0, b0)
            new_fk_s = ks.reshape(1, bs, 1, FOX_HEADS, FOX_HEAD_DIM)
            new_fv_s = vs.reshape(1, bs, 1, FOX_HEADS, FOX_HEAD_DIM)
            new_fl_s = lfs[:, :FOX_HEADS].reshape(1, bs, 1, FOX_HEADS)

        g1, b1 = row(ln_g[l, 1]), row(ln_b[l, 1])
        wq = mem_wq[l].astype(BF16)
        wo = mem_wo[l].astype(BF16)
        yp = _mem_attn(yp, wq, mk_all[l].reshape(bp, N_MEM, d), mv_all[l].reshape(bp, N_MEM, d), wo, g1, b1)
        ys = _mem_attn_sample(ys, wq, cache_mem_k, cache_mem_v, l, wo, g1, b1)

        g2, b2 = row(ln_g[l, 2]), row(ln_b[l, 2])
        wr_hi, wr_lo, br = _router_weights(moe_w_gc[l], moe_b_gc[l], moe_w_gf[l], moe_b_gf[l])
        wg, wu, wd = moe_w_gate[l].astype(BF16), moe_w_up[l].astype(BF16), moe_w_down[l].astype(BF16)
        yp = _moe_sparse(yp.reshape(bp * s, d), wr_hi, wr_lo, br, wg, wu, wd, g2, b2).reshape(bp, s, d)
        ys = _moe_dense(ys, wr_hi, wr_lo, br, wg, wu, wd, g2, b2)

    new_mk_p = mk_all.reshape(DEPTH, bp, N_MEM, MEM_HEADS, MEM_HEAD_DIM)
    new_mv_p = mv_all.reshape(DEPTH, bp, N_MEM, MEM_HEADS, MEM_HEAD_DIM)
    return (yp, ys.reshape(bs, 1, d), new_pool_p, new_pool_s, new_ret_p, new_ret_s, new_fk_p, new_fk_s,
            new_fv_p, new_fv_s, new_fl_p, new_fl_s, new_mk_p, new_mv_p)
```

```python
import functools

import numpy as np
import jax
import jax.numpy as jnp
from jax import lax
from jax.experimental import pallas as pl
from jax.experimental.pallas import tpu as pltpu

F32 = jnp.float32
BF16 = jnp.bfloat16

D_MODEL = 1024
DEPTH = 2
PAST_LEN = 8192
PAGE_SIZE = 128

POOL_WIDTH = 512
POOL_GROUP_DIM = 128
POOL_WINDOWS = (2, 4, 8, 16)
POOL_STATE = 15
POOL_HIST = 16

RET_HEADS = 4
RET_QK_DIM = 64
RET_V_DIM = 128
RET_QK = 256
RET_V = 512
RET_CHUNK = 128
ROPE_BASE = 10000.0
ROPE_HALF = RET_QK_DIM // 2

FOX_HEADS = 16
FOX_HEAD_DIM = 64

N_MEM = 256
MEM_HEADS = 4
MEM_HEAD_DIM = 256

N_GROUPS = 4
EXPERTS_PER_GROUP = 4
N_EXPERTS = 16
EXPERT_HIDDEN = 256
ROUTER_LANES = 128

DEEPNORM_ALPHA = (2 * DEPTH) ** 0.25
LN_EPS = 1e-5
LOG2E = 1.4426950408889634

VMEM_LIMIT = 56 * 1024 * 1024


def _cparams(*sem):
    return pltpu.CompilerParams(dimension_semantics=sem, vmem_limit_bytes=VMEM_LIMIT)


def _full(shape):
    n = len(shape)
    return pl.BlockSpec(shape, lambda *_: (0,) * n)


def _ln(z, g, b):
    mu = jnp.mean(z, axis=-1, keepdims=True)
    zc = z - mu
    var = jnp.mean(zc * zc, axis=-1, keepdims=True)
    return zc * lax.rsqrt(var + LN_EPS) * g + b


def _dot(a, b):
    return jnp.dot(a, b, preferred_element_type=F32)


def _dot_nt(a, b):
    return lax.dot_general(a, b, (((1,), (1,)), ((), ())), preferred_element_type=F32)


def _silu(x):
    return x * jax.nn.sigmoid(x)


def _split_bf16(x):
    hi = x.astype(BF16)
    lo = (x - hi.astype(F32)).astype(BF16)
    return hi, lo


def _split3(x):
    hi = x.astype(BF16)
    r1 = x - hi.astype(F32)
    mid = r1.astype(BF16)
    lo = (r1 - mid.astype(F32)).astype(BF16)
    return hi, mid, lo


def _qk_perm():
    p = np.arange(RET_QK)
    h = (p % 128) // ROPE_HALF
    d = (p % ROPE_HALF) + ROPE_HALF * (p // 128)
    return h * RET_QK_DIM + d


def _lane_head():
    return (np.arange(RET_QK) % 128) // ROPE_HALF


def _log_gamma():
    return np.log(1.0 - 2.0 ** (-5.0 - np.arange(RET_HEADS, dtype=np.float64)))


def _rope_tables(pos):
    inv = ROPE_BASE ** (-np.arange(ROPE_HALF, dtype=np.float64) / ROPE_HALF)
    ang = pos.astype(np.float64)[:, None] * inv[None, :]
    cos = np.tile(np.cos(ang).astype(np.float32), (1, RET_HEADS))
    sin = np.tile(np.sin(ang).astype(np.float32), (1, RET_HEADS))
    return cos, sin


def _ret_tables(chunk):
    lg = _log_gamma()
    lh = _lane_head()
    i = np.arange(chunk, dtype=np.float64)
    dist = i[:, None] - i[None, :]
    dmask = np.where(dist >= 0, np.exp(lg[:, None, None] * np.maximum(dist, 0.0)), 0.0)
    qdec = np.exp(lg[lh][None, :] * (i[:, None] + 1.0))
    kdec = np.exp(lg[lh][None, :] * (chunk - 1.0 - i[:, None]))
    sdec = np.repeat(np.exp(lg[lh] * chunk)[:, None], RET_V, axis=1)
    bmask = (lh[:, None] == (np.arange(RET_V) // RET_V_DIM)[None, :]).astype(np.float64)
    hmask = (lh[None, :] == np.arange(RET_HEADS)[:, None]).astype(np.float64)
    f = lambda a: jnp.asarray(a.astype(np.float32))
    return f(dmask), f(qdec), f(kdec), f(sdec), f(bmask), f(hmask)


def _rotary_cat(x, cos, sin):
    x1, x2 = x[:, :128], x[:, 128:]
    return jnp.concatenate([x1 * cos - x2 * sin, x1 * sin + x2 * cos], axis=1)


def _head_norm(o, gn):
    parts = []
    for h in range(RET_HEADS):
        oh = o[:, h * RET_V_DIM:(h + 1) * RET_V_DIM]
        mu = jnp.mean(oh, axis=-1, keepdims=True)
        oc = oh - mu
        var = jnp.mean(oc * oc, axis=-1, keepdims=True)
        parts.append(oc * lax.rsqrt(var + LN_EPS))
    return jnp.concatenate(parts, axis=1) * gn


def _ab_prompt_kernel(tm, x_ref, win_ref, cos_ref, sin_ref, dmask_ref, qdec_ref, kdec_ref, sdec_ref, bmask_ref,
                      hmask_ref, wpool_ref, spool_ref, gn_ref, wo_ref, lng_ref, lnb_ref,
                      y_ref, pool_ref, ret_ref, uext, state, cat):
    j = pl.program_id(1)

    @pl.when(j == 0)
    def _():
        uext[0:POOL_HIST, :] = jnp.zeros((POOL_HIST, POOL_WIDTH), F32)
        state[...] = jnp.zeros(state.shape, F32)

    x = x_ref[0]
    h = _dot(x.astype(BF16), win_ref[...])
    u = h[:, :POOL_WIDTH]
    uext[POOL_HIST:POOL_HIST + tm, :] = u

    pos = j * tm + lax.broadcasted_iota(jnp.int32, (tm, 1), 0)
    for g, w in enumerate(POOL_WINDOWS):
        lanes = slice(g * POOL_GROUP_DIM, (g + 1) * POOL_GROUP_DIM)
        win = u[:, lanes]
        for s in range(1, w):
            win = win + uext[POOL_HIST - s:POOL_HIST - s + tm, lanes]
        cnt = jnp.minimum(w, pos + 1).astype(F32)
        diff = win / cnt - u[:, lanes]
        mixed = _dot(diff.astype(BF16), wpool_ref[g]) * spool_ref[:, lanes]
        cat[:, lanes] = mixed.astype(BF16)
    tail = uext[tm:tm + POOL_HIST, :]
    uext[0:POOL_HIST, :] = tail
    pool_ref[0] = tail

    cos = cos_ref[...]
    sin = sin_ref[...]
    qr = _rotary_cat(h[:, POOL_WIDTH:POOL_WIDTH + RET_QK], cos, sin)
    kr = _rotary_cat(h[:, POOL_WIDTH + RET_QK:POOL_WIDTH + 2 * RET_QK], cos, sin) * (RET_QK_DIM ** -0.5)
    v = h[:, POOL_WIDTH + 2 * RET_QK:POOL_WIDTH + 2 * RET_QK + RET_V]
    gate = h[:, POOL_WIDTH + 2 * RET_QK + RET_V:]
    gn = gn_ref[...]
    c = RET_CHUNK
    for ci in range(tm // c):
        rows = slice(ci * c, (ci + 1) * c)
        qc, kc = qr[rows], kr[rows]
        kcb = kc.astype(BF16)
        vcb = v[rows].astype(BF16)
        sm = state[...]
        o = _dot((qc * qdec_ref[...]).astype(BF16), sm.astype(BF16))
        parts = []
        for hh in range(RET_HEADS):
            qm = (qc * hmask_ref[hh:hh + 1, :]).astype(BF16)
            sc = _dot_nt(qm, kcb) * dmask_ref[hh]
            parts.append(_dot(sc.astype(BF16), vcb[:, hh * RET_V_DIM:(hh + 1) * RET_V_DIM]))
        o = o + jnp.concatenate(parts, axis=1)
        kdt = (kc * kdec_ref[...]).T.astype(BF16)
        state[...] = sm * sdec_ref[...] + _dot(kdt, vcb) * bmask_ref[...]
        cat[rows, POOL_WIDTH:] = (_silu(gate[rows]) * _head_norm(o, gn)).astype(BF16)
    ret_ref[0] = state[...]

    mix = _dot(cat[...], wo_ref[...])
    y_ref[0] = _ln(DEEPNORM_ALPHA * x + mix, lng_ref[...], lnb_ref[...])


def _ab_prompt(x, w_in, w_pool, s_pool, gn_g, w_o, ln_g, ln_b):
    b, s, d = x.shape
    tm = min(512, s)
    cos, sin = _rope_tables(np.arange(s))
    tabs = _ret_tables(RET_CHUNK)
    nin = w_in.shape[1]
    in_specs = [
        pl.BlockSpec((1, tm, d), lambda bi, j: (bi, j, 0)),
        _full((d, nin)),
        pl.BlockSpec((tm, 128), lambda bi, j: (j, 0)),
        pl.BlockSpec((tm, 128), lambda bi, j: (j, 0)),
    ] + [_full(t.shape) for t in tabs] + [
        _full(w_pool.shape), _full(s_pool.shape), _full(gn_g.shape), _full(w_o.shape),
        _full(ln_g.shape), _full(ln_b.shape),
    ]
    out_shape = (jax.ShapeDtypeStruct((b, s, d), F32),
                 jax.ShapeDtypeStruct((b, POOL_HIST, POOL_WIDTH), F32),
                 jax.ShapeDtypeStruct((b, RET_QK, RET_V), F32))
    out_specs = (pl.BlockSpec((1, tm, d), lambda bi, j: (bi, j, 0)),
                 pl.BlockSpec((1, POOL_HIST, POOL_WIDTH), lambda bi, j: (bi, 0, 0)),
                 pl.BlockSpec((1, RET_QK, RET_V), lambda bi, j: (bi, 0, 0)))
    return pl.pallas_call(
        functools.partial(_ab_prompt_kernel, tm),
        out_shape=out_shape, grid=(b, s // tm), in_specs=in_specs, out_specs=out_specs,
        scratch_shapes=[pltpu.VMEM((POOL_HIST + tm, POOL_WIDTH), F32),
                        pltpu.VMEM((RET_QK, RET_V), F32),
                        pltpu.VMEM((tm, d), BF16)],
        compiler_params=_cparams("arbitrary", "arbitrary"), name="ab_prompt",
    )(x, w_in, jnp.asarray(cos), jnp.asarray(sin), *tabs, w_pool, s_pool, gn_g, w_o, ln_g, ln_b)


def _ab_sample_kernel(x_ref, win_ref, cos_ref, sin_ref, hmask_ref, gdec_ref, pool_ref, ret_ref,
                      wpool_ref, spool_ref, gn_ref, wo_ref, lng_ref, lnb_ref,
                      y_ref, npool_ref, nret_ref, h_s, cat_s):
    b = pl.program_id(0)
    nb = pl.num_programs(0)

    @pl.when(b == 0)
    def _():
        h_s[...] = _dot(x_ref[...].astype(BF16), win_ref[...])

    h = h_s[pl.ds(b, 1), :]
    u = h[:, :POOL_WIDTH]
    prev = pool_ref[0]
    npool_ref[0, 0:POOL_STATE - 1, :] = prev[1:POOL_STATE, :]
    npool_ref[0, POOL_STATE - 1:POOL_STATE, :] = u
    diffs = []
    for g, w in enumerate(POOL_WINDOWS):
        lanes = slice(g * POOL_GROUP_DIM, (g + 1) * POOL_GROUP_DIM)
        win = u[:, lanes] + jnp.sum(prev[POOL_STATE - (w - 1):, lanes], axis=0, keepdims=True)
        diffs.append(win / float(w) - u[:, lanes])

    cos = cos_ref[...]
    sin = sin_ref[...]
    qr = _rotary_cat(h[:, POOL_WIDTH:POOL_WIDTH + RET_QK], cos, sin)
    kr = _rotary_cat(h[:, POOL_WIDTH + RET_QK:POOL_WIDTH + 2 * RET_QK], cos, sin) * (RET_QK_DIM ** -0.5)
    v = h[:, POOL_WIDTH + 2 * RET_QK:POOL_WIDTH + 2 * RET_QK + RET_V]
    gate = h[:, POOL_WIDTH + 2 * RET_QK + RET_V:]
    hm = hmask_ref[...]
    gdec = gdec_ref[...]
    sp = ret_ref[0]
    q4 = qr * hm
    k4 = kr * hm
    sc = jnp.sum(q4 * kr, axis=-1, keepdims=True)
    v4 = jnp.concatenate([v[:, hh * RET_V_DIM:(hh + 1) * RET_V_DIM] for hh in range(RET_HEADS)]
                         + [jnp.zeros((8 - RET_HEADS, RET_V_DIM), F32)], axis=0)
    o4 = sc.astype(BF16).astype(F32) * v4.astype(BF16).astype(F32) \
        + _dot((q4 * gdec).astype(BF16), sp.astype(BF16))
    kv = lax.dot_general(k4.astype(BF16), v4.astype(BF16), (((0,), (0,)), ((), ())), preferred_element_type=F32)
    gcol = jnp.sum(hm * gdec, axis=0, keepdims=True)
    nret_ref[0] = sp * gcol.T + kv
    mu = jnp.mean(o4, axis=-1, keepdims=True)
    oc = o4 - mu
    var = jnp.mean(oc * oc, axis=-1, keepdims=True)
    on = oc * lax.rsqrt(var + LN_EPS)
    on_row = jnp.concatenate([on[hh:hh + 1, :] for hh in range(RET_HEADS)], axis=1) * gn_ref[...]
    cat_s[pl.ds(b, 1), :] = jnp.concatenate(diffs + [_silu(gate) * on_row], axis=1)

    @pl.when(b == nb - 1)
    def _():
        c = cat_s[...]
        mixed = [_dot(c[:, g * POOL_GROUP_DIM:(g + 1) * POOL_GROUP_DIM].astype(BF16), wpool_ref[g])
                 * spool_ref[:, g * POOL_GROUP_DIM:(g + 1) * POOL_GROUP_DIM] for g in range(len(POOL_WINDOWS))]
        cat = jnp.concatenate(mixed + [c[:, POOL_WIDTH:]], axis=1).astype(BF16)
        mix = _dot(cat, wo_ref[...])
        y_ref[...] = _ln(DEEPNORM_ALPHA * x_ref[...] + mix, lng_ref[...], lnb_ref[...])


def _ab_sample(x2d, w_in, pool_prev, ret_prev_perm, w_pool, s_pool, gn_g, w_o, ln_g, ln_b):
    b, d = x2d.shape
    cos, sin = _rope_tables(np.array([PAST_LEN]))
    hmask = np.zeros((8, RET_QK), np.float32)
    hmask[:RET_HEADS] = (_lane_head()[None, :] == np.arange(RET_HEADS)[:, None])
    gdec = np.exp(_log_gamma())[_lane_head()][None, :].astype(np.float32)
    per_b = lambda shape: pl.BlockSpec((1,) + shape, lambda i: (i, 0, 0))
    in_specs = [_full((b, d)), _full(w_in.shape), _full((1, 128)), _full((1, 128)), _full((8, RET_QK)),
                _full((1, RET_QK)), per_b((POOL_STATE, POOL_WIDTH)), per_b((RET_QK, RET_V_DIM)),
                _full(w_pool.shape), _full(s_pool.shape), _full(gn_g.shape), _full(w_o.shape),
                _full(ln_g.shape), _full(ln_b.shape)]
    out_shape = (jax.ShapeDtypeStruct((b, d), F32),
                 jax.ShapeDtypeStruct((b, POOL_STATE, POOL_WIDTH), F32),
                 jax.ShapeDtypeStruct((b, RET_QK, RET_V_DIM), F32))
    out_specs = (_full((b, d)), per_b((POOL_STATE, POOL_WIDTH)), per_b((RET_QK, RET_V_DIM)))
    return pl.pallas_call(
        _ab_sample_kernel, out_shape=out_shape, grid=(b,), in_specs=in_specs, out_specs=out_specs,
        scratch_shapes=[pltpu.VMEM((b, w_in.shape[1]), F32), pltpu.VMEM((b, d), F32)],
        compiler_params=_cparams("arbitrary"), name="ab_sample",
    )(x2d, w_in, jnp.asarray(cos), jnp.asarray(sin), jnp.asarray(hmask), jnp.asarray(gdec),
      pool_prev, ret_prev_perm, w_pool, s_pool, gn_g, w_o, ln_g, ln_b)


def _mem_kv_kernel(m_ref, w_ref, k_ref, v_ref, kb_ref, vb_ref):
    h = _dot(m_ref[...].astype(BF16), w_ref[0])
    kb_ref[0] = h[:, :D_MODEL].astype(BF16)
    vb_ref[0] = h[:, D_MODEL:].astype(BF16)
    bsz = k_ref.shape[1]
    for hh in range(MEM_HEADS):
        lanes = slice(hh * MEM_HEAD_DIM, (hh + 1) * MEM_HEAD_DIM)
        k_ref[0, :, :, hh, :] = h[:, lanes].reshape(bsz, N_MEM, MEM_HEAD_DIM)
        v_ref[0, :, :, hh, :] = h[:, D_MODEL + hh * MEM_HEAD_DIM:D_MODEL + (hh + 1) * MEM_HEAD_DIM].reshape(
            bsz, N_MEM, MEM_HEAD_DIM)


def _mem_kv(mem2d, w_kv, bsz):
    n, d = mem2d.shape
    depth = w_kv.shape[0]
    cache = jax.ShapeDtypeStruct((depth, bsz, N_MEM, MEM_HEADS, MEM_HEAD_DIM), F32)
    flat = jax.ShapeDtypeStruct((depth, n, d), BF16)
    cspec = pl.BlockSpec((1, bsz, N_MEM, MEM_HEADS, MEM_HEAD_DIM), lambda l: (l, 0, 0, 0, 0))
    fspec = pl.BlockSpec((1, n, d), lambda l: (l, 0, 0))
    return pl.pallas_call(
        _mem_kv_kernel, out_shape=(cache, cache, flat, flat), grid=(depth,),
        in_specs=[_full((n, d)), pl.BlockSpec((1, d, 2 * d), lambda l: (l, 0, 0))],
        out_specs=(cspec, cspec, fspec, fspec),
        compiler_params=_cparams("arbitrary"), name="mem_kv",
    )(mem2d, w_kv)


def _mem_attn_kernel(x_ref, wq_ref, k_ref, v_ref, wo_ref, lng_ref, lnb_ref, y_ref):
    x = x_ref[0]
    q = _dot(x.astype(BF16), wq_ref[...]) * (MEM_HEAD_DIM ** -0.5)
    parts = []
    for h in range(MEM_HEADS):
        lanes = slice(h * MEM_HEAD_DIM, (h + 1) * MEM_HEAD_DIM)
        logits = _dot_nt(q[:, lanes].astype(BF16), k_ref[0, :, lanes].astype(BF16))
        m = jnp.max(logits, axis=-1, keepdims=True)
        p = jnp.exp(logits - m)
        l = jnp.sum(p, axis=-1, keepdims=True)
        parts.append(_dot(p.astype(BF16), v_ref[0, :, lanes].astype(BF16)) / l)
    o = jnp.concatenate(parts, axis=1)
    mix = _dot(o.astype(BF16), wo_ref[...])
    y_ref[0] = _ln(DEEPNORM_ALPHA * x + mix, lng_ref[...], lnb_ref[...])


def _mem_attn(x, w_q, k, v, w_o, ln_g, ln_b):
    b, s, d = x.shape
    tm = min(512, s)
    xs = pl.BlockSpec((1, tm, d), lambda bi, j: (bi, j, 0))
    kvs = pl.BlockSpec((1, N_MEM, d), lambda bi, j: (bi, 0, 0))
    return pl.pallas_call(
        _mem_attn_kernel, out_shape=jax.ShapeDtypeStruct((b, s, d), F32), grid=(b, s // tm),
        in_specs=[xs, _full(w_q.shape), kvs, kvs, _full(w_o.shape), _full(ln_g.shape), _full(ln_b.shape)],
        out_specs=xs, compiler_params=_cparams("arbitrary", "arbitrary"), name="mem_attn",
    )(x, w_q, k, v, w_o, ln_g, ln_b)


def _mem_attn_sample_kernel(x_ref, wq_ref, k_ref, v_ref, wo_ref, lng_ref, lnb_ref, y_ref, q_s, o_s):
    b = pl.program_id(0)
    nb = pl.num_programs(0)

    @pl.when(b == 0)
    def _():
        q_s[...] = _dot(x_ref[...].astype(BF16), wq_ref[...]) * (MEM_HEAD_DIM ** -0.5)

    qrow = q_s[pl.ds(b, 1), :]
    q4 = jnp.concatenate([qrow[:, h * MEM_HEAD_DIM:(h + 1) * MEM_HEAD_DIM] for h in range(MEM_HEADS)]
                         + [jnp.zeros((8 - MEM_HEADS, MEM_HEAD_DIM), F32)], axis=0)
    k2 = k_ref[0, 0].reshape(N_MEM * MEM_HEADS, MEM_HEAD_DIM).astype(BF16)
    v2 = v_ref[0, 0].reshape(N_MEM * MEM_HEADS, MEM_HEAD_DIM).astype(BF16)
    logits = _dot_nt(q4.astype(BF16), k2)
    col = lax.broadcasted_iota(jnp.int32, logits.shape, 1)
    rowi = lax.broadcasted_iota(jnp.int32, logits.shape, 0)
    own = (col % MEM_HEADS) == rowi
    m = jnp.max(jnp.where(own, logits, -jnp.inf), axis=-1, keepdims=True)
    m = jnp.where(rowi[:, :1] < MEM_HEADS, m, 0.0)
    p = jnp.where(own, jnp.exp(logits - m), 0.0)
    l = jnp.maximum(jnp.sum(p, axis=-1, keepdims=True), jnp.where(rowi[:, :1] < MEM_HEADS, 0.0, 1.0))
    o4 = _dot(p.astype(BF16), v2) / l
    o_s[pl.ds(b, 1), :] = jnp.concatenate([o4[h:h + 1, :] for h in range(MEM_HEADS)], axis=1)

    @pl.when(b == nb - 1)
    def _():
        mix = _dot(o_s[...].astype(BF16), wo_ref[...])
        y_ref[...] = _ln(DEEPNORM_ALPHA * x_ref[...] + mix, lng_ref[...], lnb_ref[...])


def _mem_attn_sample(x2d, w_q, cache_k, cache_v, layer, w_o, ln_g, ln_b):
    n, d = x2d.shape
    kvs = pl.BlockSpec((1, 1, N_MEM, MEM_HEADS, MEM_HEAD_DIM), lambda b: (layer, b, 0, 0, 0))
    return pl.pallas_call(
        _mem_attn_sample_kernel, out_shape=jax.ShapeDtypeStruct((n, d), F32), grid=(n,),
        in_specs=[_full((n, d)), _full(w_q.shape), kvs, kvs, _full(w_o.shape), _full(ln_g.shape), _full(ln_b.shape)],
        out_specs=_full((n, d)),
        scratch_shapes=[pltpu.VMEM((n, d), F32), pltpu.VMEM((n, d), F32)],
        compiler_params=_cparams("arbitrary"), name="mem_attn_sample",
    )(x2d, w_q, cache_k, cache_v, w_o, ln_g, ln_b)


def _router_gates(x, wr_hi, wr_lo, br):
    xh, xl = _split_bf16(x)
    lg = _dot(xh, wr_hi) + (_dot(xh, wr_lo) + _dot(xl, wr_hi)) + br
    n = x.shape[0]
    lane_i = lax.broadcasted_iota(jnp.int32, (n, ROUTER_LANES), 1)
    lane = lane_i.astype(F32)
    is_c = lane_i < N_GROUPS
    neg = jnp.float32(-jnp.inf)
    big = jnp.float32(ROUTER_LANES)
    coarse = jnp.where(is_c, lg, neg)
    cmax = jnp.max(coarse, axis=-1, keepdims=True)
    grp = jnp.min(jnp.where(coarse == cmax, lane, big), axis=-1, keepdims=True)
    p_grp = 1.0 / jnp.sum(jnp.where(is_c, jnp.exp(lg - cmax), 0.0), axis=-1, keepdims=True)
    lane_grp = ((lane_i - N_GROUPS) // EXPERTS_PER_GROUP).astype(F32)
    in_grp = (lane_i >= N_GROUPS) & (lane_i < N_GROUPS + N_EXPERTS) & (lane_grp == grp)
    f1 = jnp.where(in_grp, lg, neg)
    t1 = jnp.max(f1, axis=-1, keepdims=True)
    i1 = jnp.min(jnp.where(f1 == t1, lane, big), axis=-1, keepdims=True)
    f2 = jnp.where(lane == i1, neg, f1)
    t2 = jnp.max(f2, axis=-1, keepdims=True)
    i2 = jnp.min(jnp.where(f2 == t2, lane, big), axis=-1, keepdims=True)
    e2 = jnp.exp(t2 - t1)
    w1 = p_grp / (1.0 + e2)
    w2 = p_grp * e2 / (1.0 + e2)
    return jnp.where(lane == i1, w1, 0.0) + jnp.where(lane == i2, w2, 0.0) + jnp.where(lane == grp, 1.0, 0.0)


MOE_TILE = 1024
MOE_SUB = 512
MOE_CHUNK = 160


def _moe_sparse_kernel(x_ref, wrh_ref, wrl_ref, br_ref, tri_ref, wg_ref, wu_ref, wd_ref, lng_ref, lnb_ref, y_ref,
                       xb, gates3, onehot, rank, cnt, acc):
    g = pl.program_id(1)
    tm = x_ref.shape[0]
    nsub = tm // MOE_SUB
    lane = lax.broadcasted_iota(jnp.int32, (MOE_SUB, ROUTER_LANES), 1)

    @pl.when(g == 0)
    def _():
        x = x_ref[...]
        xb[...] = x.astype(BF16)
        gt = _router_gates(x, wrh_ref[...], wrl_ref[...], br_ref[...])
        hi, mid, lo = _split3(gt)
        gates3[0] = hi
        gates3[1] = mid
        gates3[2] = lo
        lane_t = lax.broadcasted_iota(jnp.int32, gt.shape, 1)
        oh = jnp.where(lane_t < N_GROUPS, gt, 0.0).astype(BF16)
        onehot[...] = oh
        cnt[...] = jnp.zeros(cnt.shape, F32)
        for st in range(nsub):
            rows = slice(st * MOE_SUB, (st + 1) * MOE_SUB)
            pref = _dot(tri_ref[...], oh[rows])
            rank[rows, :] = pref - 1.0
            cnt[st:st + 1, :] = pref[MOE_SUB - 1:, :]
        acc[...] = jnp.zeros(acc.shape, F32)

    lane_c = lax.broadcasted_iota(jnp.int32, (MOE_SUB, MOE_CHUNK), 1).astype(F32)
    lane_cnt = lax.broadcasted_iota(jnp.int32, cnt.shape, 1)
    most = jnp.max(jnp.where(lane_cnt == g, cnt[...], 0.0))
    for c in range(-(-MOE_SUB // MOE_CHUNK)):
        @pl.when(most > float(c * MOE_CHUNK))
        def _(c=c):
            sels, xs, gs = [], [], []
            for st in range(nsub):
                rows = slice(st * MOE_SUB, (st + 1) * MOE_SUB)
                member = jnp.sum(jnp.where(lane == g, onehot[rows, :].astype(F32), 0.0), axis=-1, keepdims=True)
                rk = jnp.sum(jnp.where(lane == g, rank[rows, :], 0.0), axis=-1, keepdims=True) - float(c * MOE_CHUNK)
                sel = jnp.where((member > 0.0) & (rk == lane_c), 1.0, 0.0).astype(BF16)
                sels.append(sel)
                tn = lambda a, sel=sel: lax.dot_general(sel, a, (((0,), (0,)), ((), ())), preferred_element_type=F32)
                xs.append(tn(xb[rows, :]).astype(BF16))
                gs.append(tn(gates3[0, rows, :]) + (tn(gates3[1, rows, :]) + tn(gates3[2, rows, :])))
            xg = jnp.concatenate(xs, axis=0)
            gg = jnp.concatenate(gs, axis=0)
            lane_g = lax.broadcasted_iota(jnp.int32, gg.shape, 1)
            y = jnp.zeros((nsub * MOE_CHUNK, x_ref.shape[1]), F32)
            for e in range(EXPERTS_PER_GROUP):
                gcol = jnp.sum(jnp.where(lane_g == N_GROUPS + EXPERTS_PER_GROUP * g + e, gg, 0.0),
                               axis=-1, keepdims=True)
                hid = _silu(_dot(xg, wg_ref[0, e])) * _dot(xg, wu_ref[0, e]) * gcol
                y = y + _dot(hid.astype(BF16), wd_ref[0, e])
            yh, yl = _split_bf16(y)
            for st in range(nsub):
                rows = slice(st * MOE_SUB, (st + 1) * MOE_SUB)
                part = slice(st * MOE_CHUNK, (st + 1) * MOE_CHUNK)
                acc[rows, :] += _dot(sels[st], yh[part]) + _dot(sels[st], yl[part])

    @pl.when(g == N_GROUPS - 1)
    def _():
        y_ref[...] = _ln(DEEPNORM_ALPHA * x_ref[...] + acc[...], lng_ref[...], lnb_ref[...])


def _moe_sparse(x2d, wr_hi, wr_lo, br, w_gate, w_up, w_down, ln_g, ln_b):
    n, d = x2d.shape
    tm = MOE_TILE
    assert n % tm == 0 and tm % MOE_SUB == 0
    tri = jnp.asarray(np.tril(np.ones((MOE_SUB, MOE_SUB), np.float32)), dtype=BF16)
    grouped = lambda w: w.reshape(N_GROUPS, EXPERTS_PER_GROUP, w.shape[1], w.shape[2])
    xs = pl.BlockSpec((tm, d), lambda i, g: (i, 0))
    wspec = lambda a, b: pl.BlockSpec((1, EXPERTS_PER_GROUP, a, b), lambda i, g: (g, 0, 0, 0))
    return pl.pallas_call(
        _moe_sparse_kernel, out_shape=jax.ShapeDtypeStruct((n, d), F32), grid=(n // tm, N_GROUPS),
        in_specs=[xs, _full(wr_hi.shape), _full(wr_lo.shape), _full(br.shape), _full(tri.shape),
                  wspec(d, EXPERT_HIDDEN), wspec(d, EXPERT_HIDDEN), wspec(EXPERT_HIDDEN, d),
                  _full(ln_g.shape), _full(ln_b.shape)],
        out_specs=xs,
        scratch_shapes=[pltpu.VMEM((tm, d), BF16), pltpu.VMEM((3, tm, ROUTER_LANES), BF16),
                        pltpu.VMEM((tm, ROUTER_LANES), BF16), pltpu.VMEM((tm, ROUTER_LANES), F32),
                        pltpu.VMEM((8, ROUTER_LANES), F32), pltpu.VMEM((tm, d), F32)],
        compiler_params=_cparams("arbitrary", "arbitrary"), name="moe_sparse",
    )(x2d, wr_hi, wr_lo, br, tri, grouped(w_gate), grouped(w_up), grouped(w_down), ln_g, ln_b)


def _moe_dense_kernel(x_ref, wrh_ref, wrl_ref, br_ref, wg_ref, wu_ref, wd_ref, lng_ref, lnb_ref, y_ref,
                      xb, gates, acc):
    e = pl.program_id(1)

    @pl.when(e == 0)
    def _():
        x = x_ref[...]
        xb[...] = x.astype(BF16)
        gates[...] = _router_gates(x, wrh_ref[...], wrl_ref[...], br_ref[...])
        acc[...] = jnp.zeros(acc.shape, F32)

    lane_i = lax.broadcasted_iota(jnp.int32, gates.shape, 1)
    gcol = jnp.sum(jnp.where(lane_i == e + N_GROUPS, gates[...], 0.0), axis=-1, keepdims=True)
    xv = xb[...]
    hid = _silu(_dot(xv, wg_ref[0])) * _dot(xv, wu_ref[0]) * gcol
    acc[...] += _dot(hid.astype(BF16), wd_ref[0])

    @pl.when(e == N_EXPERTS - 1)
    def _():
        y_ref[...] = _ln(DEEPNORM_ALPHA * x_ref[...] + acc[...], lng_ref[...], lnb_ref[...])


def _moe_dense(x2d, wr_hi, wr_lo, br, w_gate, w_up, w_down, ln_g, ln_b):
    n, d = x2d.shape
    tm = min(1024, n)
    xs = pl.BlockSpec((tm, d), lambda i, e: (i, 0))
    return pl.pallas_call(
        _moe_dense_kernel, out_shape=jax.ShapeDtypeStruct((n, d), F32), grid=(n // tm, N_EXPERTS),
        in_specs=[xs, _full(wr_hi.shape), _full(wr_lo.shape), _full(br.shape),
                  pl.BlockSpec((1, d, EXPERT_HIDDEN), lambda i, e: (e, 0, 0)),
                  pl.BlockSpec((1, d, EXPERT_HIDDEN), lambda i, e: (e, 0, 0)),
                  pl.BlockSpec((1, EXPERT_HIDDEN, d), lambda i, e: (e, 0, 0)),
                  _full(ln_g.shape), _full(ln_b.shape)],
        out_specs=xs,
        scratch_shapes=[pltpu.VMEM((tm, d), BF16), pltpu.VMEM((tm, ROUTER_LANES), F32), pltpu.VMEM((tm, d), F32)],
        compiler_params=_cparams("arbitrary", "arbitrary"), name="moe_dense",
    )(x2d, wr_hi, wr_lo, br, w_gate, w_up, w_down, ln_g, ln_b)


def _log_sigmoid(z):
    return jnp.minimum(z, 0.0) - jnp.log1p(jnp.exp(-jnp.abs(z)))


FOX_AUG = 2 * FOX_HEADS * FOX_HEAD_DIM
FOX_TILE = 512


def _fox_aug_tables():
    pq = np.zeros((128, FOX_AUG), np.float32)
    pk = np.zeros((128, FOX_AUG), np.float32)
    cq = np.zeros((1, FOX_AUG), np.float32)
    ck = np.zeros((1, FOX_AUG), np.float32)
    cv = np.zeros((1, FOX_AUG), np.float32)
    dm = np.zeros((1, FOX_AUG), np.float32)
    for h in range(FOX_HEADS):
        base = 128 * h
        off = base + (FOX_HEAD_DIM if h % 2 == 0 else 0)
        data = base + (0 if h % 2 == 0 else FOX_HEAD_DIM)
        dm[0, data:data + FOX_HEAD_DIM] = 1.0
        for g in range(3):
            pq[16 * g + h, off + g] = 1.0
            cq[0, off + 3 + g] = 1.0
            ck[0, off + g] = 1.0
            pk[16 * g + h, off + 3 + g] = -1.0
        cv[0, off] = 1.0
    return pq, pk, cq, ck, cv.T.copy(), dm, dm.T.copy()


def _fox_proj_kernel(x_ref, w_ref, wvt_ref, wf_ref, bf_ref, tri_ref, pq_ref, pk_ref, cq_ref, ck_ref, cvt_ref,
                     dm_ref, dmt_ref, qa_ref, ka_ref, vat_ref, k_ref, v_ref, lf_ref, carry):
    j = pl.program_id(1)

    @pl.when(j == 0)
    def _():
        carry[...] = jnp.zeros(carry.shape, F32)

    xb = x_ref[0].astype(BF16)
    h = _dot(xb, w_ref[...])
    q2 = h[:, :D_MODEL] * (FOX_HEAD_DIM ** -0.5 * LOG2E)
    k = h[:, D_MODEL:2 * D_MODEL]
    k_ref[0] = k
    v_ref[0] = h[:, 2 * D_MODEL:]
    vt = _dot_nt(wvt_ref[...], xb)
    lf = _log_sigmoid(_dot(xb, wf_ref[...]) + bf_ref[...])
    lf_ref[0] = lf[:, :FOX_HEADS]
    hi, mid, lo = _split3(lf)
    tri = tri_ref[...]
    c = _dot(tri, hi) + (_dot(tri, mid) + _dot(tri, lo)) + carry[...]
    carry[...] = c[c.shape[0] - 1:, :]
    chi, cmid, clo = _split3(c * LOG2E)
    lane = lax.broadcasted_iota(jnp.int32, c.shape, 1)
    c3 = jnp.where(lane < 16, chi.astype(F32), jnp.where(lane < 32, cmid.astype(F32), clo.astype(F32))).astype(BF16)
    qb = _dot(c3, pq_ref[...]) + cq_ref[...]
    kb = _dot(c3, pk_ref[...]) + ck_ref[...]
    for hh in range(FOX_HEADS):
        blk = slice(128 * hh, 128 * hh + 128)
        src = slice(128 * (hh // 2), 128 * (hh // 2) + 128)
        is_data = dm_ref[:, blk] > 0.0
        qa_ref[0, :, blk] = jnp.where(is_data, q2[:, src], qb[:, blk]).astype(BF16)
        ka_ref[0, :, blk] = jnp.where(is_data, k[:, src], kb[:, blk]).astype(BF16)
        vat_ref[0, blk, :] = jnp.where(dmt_ref[blk, :] > 0.0, vt[src, :], cvt_ref[blk, :]).astype(BF16)


def _fox_proj(x, w_qkv, w_vt, w_f3, b_f3):
    b, s, d = x.shape
    tm = min(FOX_TILE, s)
    tri = jnp.asarray(np.tril(np.ones((tm, tm), np.float32)), dtype=BF16)
    pq, pk, cq, ck, cvt, dm, dmt = _fox_aug_tables()
    consts = [jnp.asarray(pq, dtype=BF16), jnp.asarray(pk, dtype=BF16), jnp.asarray(cq), jnp.asarray(ck),
              jnp.asarray(cvt), jnp.asarray(dm), jnp.asarray(dmt)]
    xs = pl.BlockSpec((1, tm, d), lambda bi, j: (bi, j, 0))
    augs = pl.BlockSpec((1, tm, FOX_AUG), lambda bi, j: (bi, j, 0))
    f32o = jax.ShapeDtypeStruct((b, s, d), F32)
    augo = jax.ShapeDtypeStruct((b, s, FOX_AUG), BF16)
    return pl.pallas_call(
        _fox_proj_kernel,
        out_shape=(augo, augo, jax.ShapeDtypeStruct((b, FOX_AUG, s), BF16), f32o, f32o,
                   jax.ShapeDtypeStruct((b, s, FOX_HEADS), F32)),
        grid=(b, s // tm),
        in_specs=[xs, _full(w_qkv.shape), _full(w_vt.shape), _full(w_f3.shape), _full(b_f3.shape), _full(tri.shape)]
        + [_full(c.shape) for c in consts],
        out_specs=(augs, augs, pl.BlockSpec((1, FOX_AUG, tm), lambda bi, j: (bi, 0, j)), xs, xs,
                   pl.BlockSpec((1, tm, FOX_HEADS), lambda bi, j: (bi, j, 0))),
        scratch_shapes=[pltpu.VMEM((1, 128), F32)],
        compiler_params=_cparams("arbitrary", "arbitrary"), name="fox_proj",
    )(x, w_qkv, w_vt, w_f3, b_f3, tri, *consts)


def _fox_attn_body(tq, i, q_ref, k_ref, vt_ref, o_ref, m_s, acc_s, s_buf):
    krow = lax.broadcasted_iota(jnp.int32, (tq, tq), 0)
    qcol = lax.broadcasted_iota(jnp.int32, (tq, tq), 1)
    m_s[...] = jnp.full(m_s.shape, -jnp.inf, F32)
    acc_s[...] = jnp.zeros(acc_s.shape, F32)
    heads = [slice(128 * hd, 128 * hd + 128) for hd in range(2)]

    def scores(jj, diagonal):
        k0 = pl.multiple_of(jj * tq, tq)
        sts = [_dot_nt(k_ref[0, pl.ds(k0, tq), lanes], q_ref[0, :, lanes]) for lanes in heads]
        if diagonal:
            sts = [jnp.where(krow <= qcol, st, -jnp.inf) for st in sts]
        return sts

    def consume(jj):
        k0 = pl.multiple_of(jj * tq, tq)
        for hd, lanes in enumerate(heads):
            st = s_buf[hd]
            m_old = m_s[hd]
            m_new = jnp.maximum(m_old, jnp.max(st, axis=0, keepdims=True))
            pt = jnp.exp2(st - m_new).astype(BF16)
            acc_s[hd] = jnp.exp2(m_old - m_new) * acc_s[hd] + _dot(vt_ref[0, lanes, pl.ds(k0, tq)], pt)
            m_s[hd] = m_new

    def step(jj, diagonal_next):
        nxt = scores(jj + 1, diagonal_next)
        consume(jj)
        for hd in range(2):
            s_buf[hd] = nxt[hd]

    @pl.when(i == 0)
    def _():
        for hd, st in enumerate(scores(0, True)):
            s_buf[hd] = st

    @pl.when(i > 0)
    def _():
        for hd, st in enumerate(scores(0, False)):
            s_buf[hd] = st

        def body(jj, carry):
            step(jj, False)
            return carry

        lax.fori_loop(0, i - 1, body, 0)
        step(i - 1, True)

    consume(i)
    outs = []
    for hd in range(2):
        acc = acc_s[hd]
        l_row = FOX_HEAD_DIM if hd == 0 else 0
        outs.append(acc / acc[l_row:l_row + 1, :])
    row = lax.broadcasted_iota(jnp.int32, (128, tq), 0)
    o_ref[0] = jnp.where(row < FOX_HEAD_DIM, outs[0], outs[1]).T.astype(BF16)


def _proj_ln_kernel(o_ref, x_ref, w_ref, lng_ref, lnb_ref, y_ref):
    mix = _dot(o_ref[...].astype(BF16), w_ref[...])
    y_ref[...] = _ln(DEEPNORM_ALPHA * x_ref[...] + mix, lng_ref[...], lnb_ref[...])


def _proj_ln(o2d, x2d, w, ln_g, ln_b):
    n, d = x2d.shape
    tm = min(1024, n)
    xs = pl.BlockSpec((tm, d), lambda i: (i, 0))
    return pl.pallas_call(
        _proj_ln_kernel, out_shape=jax.ShapeDtypeStruct((n, d), F32), grid=(n // tm,),
        in_specs=[xs, xs, _full(w.shape), _full(ln_g.shape), _full(ln_b.shape)], out_specs=xs,
        compiler_params=_cparams("arbitrary"), name="proj_ln",
    )(o2d, x2d, w, ln_g, ln_b)


def _fox_proj_sample_kernel(x_ref, w_ref, wf_ref, bf_ref, q_ref, k_ref, v_ref, lf_ref):
    xb = x_ref[...].astype(BF16)
    h = _dot(xb, w_ref[...])
    q_ref[...] = h[:, :D_MODEL] * (FOX_HEAD_DIM ** -0.5)
    k_ref[...] = h[:, D_MODEL:2 * D_MODEL]
    v_ref[...] = h[:, 2 * D_MODEL:]
    lf_ref[...] = _log_sigmoid(_dot(xb, wf_ref[...]) + bf_ref[...])


def _fox_proj_sample(x2d, w_qkv, w_f, b_f):
    n, d = x2d.shape
    o = jax.ShapeDtypeStruct((n, d), F32)
    return pl.pallas_call(
        _fox_proj_sample_kernel, out_shape=(o, o, o, jax.ShapeDtypeStruct((n, 128), F32)),
        name="fox_proj_sample", compiler_params=pltpu.CompilerParams(vmem_limit_bytes=VMEM_LIMIT),
    )(x2d, w_qkv, w_f, b_f)


PAGES_PER_STEP = 8


def _fox_paged_body(s, ns, q_ref, kn_ref, vn_ref, lfn_ref, hm_ref, ut_ref, k_refs, v_refs, lf_refs, o_ref,
                    qb, m_s, l_s, acc_s, ccar):
    npp = len(k_refs)
    nh, hd = FOX_HEADS, FOX_HEAD_DIM

    @pl.when(s == 0)
    def _():
        qb[...] = jnp.broadcast_to(q_ref[0], (128, D_MODEL)).T
        m_s[...] = jnp.full(m_s.shape, -jnp.inf, F32)
        l_s[...] = jnp.zeros(l_s.shape, F32)
        acc_s[...] = jnp.zeros(acc_s.shape, F32)
        ccar[...] = jnp.zeros(ccar.shape, F32)

    lf = jnp.concatenate([r[0] for r in lf_refs], axis=0)
    hi, mid, lo = _split3(lf)
    ut = ut_ref[...]
    cs_tot = _dot(hi, ut) + (_dot(mid, ut) + _dot(lo, ut))
    base = ccar[...]
    svals = []
    partial = [[None] * nh for _ in range(npp)]
    for h in range(nh):
        hrows = slice(h * hd, (h + 1) * hd)
        qh = qb[hrows, :]
        for p in range(npp):
            partial[p][h] = jnp.sum((k_refs[p][0, hrows, :] * qh).reshape(hd // 8, 8, 128), axis=0)
    for p in range(npp):
        rows = slice(p * nh, (p + 1) * nh)
        c_p = cs_tot[rows, :128] + base
        base = base + cs_tot[rows, 128:]
        qk = jnp.sum(jnp.concatenate(partial[p], axis=0).reshape(nh, 8, 128), axis=1)
        svals.append(qk - c_p)
    ccar[...] = base

    smax = svals[0]
    for p in range(1, npp):
        smax = jnp.maximum(smax, svals[p])
    m_old = m_s[...]
    m_new = jnp.maximum(m_old, jnp.max(smax, axis=-1, keepdims=True))
    alpha = jnp.exp(m_old - m_new)
    ps = [jnp.exp(sv - m_new) for sv in svals]
    psum = ps[0]
    for p in range(1, npp):
        psum = psum + ps[p]
    l_s[...] = alpha * l_s[...] + jnp.sum(psum, axis=-1, keepdims=True)
    m_s[...] = m_new

    def rows64(x):
        return jnp.concatenate([jnp.broadcast_to(x[h:h + 1, :], (hd, 128)) for h in range(nh)], axis=0)

    for h in range(nh):
        hrows = slice(h * hd, (h + 1) * hd)
        a = acc_s[hrows, :] * jnp.broadcast_to(alpha[h:h + 1, :], (hd, 128))
        for p in range(npp):
            a = a + jnp.broadcast_to(ps[p][h:h + 1, :], (hd, 128)) * v_refs[p][0, hrows, :]
        acc_s[hrows, :] = a

    @pl.when(s == ns - 1)
    def _():
        hm = hm_ref[...]
        lane = lax.broadcasted_iota(jnp.int32, (nh, 128), 1)
        head = lax.broadcasted_iota(jnp.int32, (nh, 128), 0)
        lf_col = jnp.sum(jnp.where(lane == head, jnp.broadcast_to(lfn_ref[0], (nh, 128)), 0.0),
                         axis=-1, keepdims=True)
        qk = jnp.broadcast_to(q_ref[0] * kn_ref[0], (nh, D_MODEL)) * hm
        s_new = jnp.sum(qk, axis=-1, keepdims=True) - (ccar[...] + lf_col)
        m_prev = m_s[...]
        m_fin = jnp.maximum(m_prev, s_new)
        a2 = jnp.exp(m_prev - m_fin)
        pn = jnp.exp(s_new - m_fin)
        l_fin = a2 * l_s[...] + pn
        vnb = jnp.broadcast_to(vn_ref[0], (128, D_MODEL)).T
        lane_w = lax.broadcasted_iota(jnp.int32, (D_MODEL, 128), 1)
        tot = acc_s[...] * rows64(a2) + jnp.where(lane_w == 0, rows64(pn) * vnb, 0.0)
        o_col = jnp.sum(tot, axis=-1, keepdims=True) / rows64(l_fin)
        o_ref[0] = o_col.T[0:1, :]


def _fox_attn_paged_kernel(tq, npp, pt_ref, qa_ref, ka_ref, vat_ref, q_ref, kn_ref, vn_ref, lfn_ref, hm_ref, ut_ref,
                           *rest):
    del pt_ref
    k_refs, v_refs, lf_refs = rest[:npp], rest[npp:2 * npp], rest[2 * npp:3 * npp]
    o_att, o_smp = rest[3 * npp:3 * npp + 2]
    m_a, acc_a, s_buf, qb, m_p, l_p, acc_p, ccar = rest[3 * npp + 2:]
    i = pl.program_id(2)
    _fox_attn_body(tq, i, qa_ref, ka_ref, vat_ref, o_att, m_a, acc_a, s_buf)
    _fox_paged_body(i, pl.num_programs(2), q_ref, kn_ref, vn_ref, lfn_ref, hm_ref, ut_ref, k_refs, v_refs, lf_refs,
                    o_smp, qb, m_p, l_p, acc_p, ccar)


def _fox_attn_paged(qa, ka, vat, q, k_new, v_new, lf_new, cache_kt, cache_vt, cache_lft, page_table):
    bp, s, _ = qa.shape
    bs, d = q.shape
    tq = min(FOX_TILE, s)
    nq = s // tq
    n_pages = page_table.shape[1]
    npp = PAGES_PER_STEP
    hm = np.zeros((FOX_HEADS, d), np.float32)
    for h in range(FOX_HEADS):
        hm[h, h * FOX_HEAD_DIM:(h + 1) * FOX_HEAD_DIM] = 1.0
    ut = np.concatenate([np.triu(np.ones((128, 128), np.float32)), np.ones((128, 128), np.float32)], axis=1)
    assert bs == bp * (FOX_HEADS // 2) and n_pages == npp * nq, (bs, bp, n_pages, nq)
    seq = lambda bi, hp: bi * (FOX_HEADS // 2) + hp
    row = lambda w: pl.BlockSpec((1, 1, w), lambda bi, hp, i, pt: (seq(bi, hp), 0, 0))
    const = lambda shape: pl.BlockSpec(shape, lambda bi, hp, i, pt: (0,) * len(shape))

    def page_spec(p, r):
        return pl.BlockSpec((1, r, PAGE_SIZE),
                            lambda bi, hp, i, pt: (pt[seq(bi, hp) * n_pages + i * npp + p], 0, 0))

    qs = pl.BlockSpec((1, tq, 256), lambda bi, hp, i, pt: (bi, i, hp))
    in_specs = ([qs, pl.BlockSpec((1, s, 256), lambda bi, hp, i, pt: (bi, 0, hp)),
                 pl.BlockSpec((1, 256, s), lambda bi, hp, i, pt: (bi, hp, 0)),
                 row(d), row(d), row(d), row(128), const((FOX_HEADS, d)), const((128, 256))]
                + [page_spec(p, d) for p in range(npp)] + [page_spec(p, d) for p in range(npp)]
                + [page_spec(p, FOX_HEADS) for p in range(npp)])
    grid_spec = pltpu.PrefetchScalarGridSpec(
        num_scalar_prefetch=1, grid=(bp, FOX_HEADS // 2, nq), in_specs=in_specs,
        out_specs=(pl.BlockSpec((1, tq, 128), lambda bi, hp, i, pt: (bi, i, hp)), row(d)),
        scratch_shapes=[pltpu.VMEM((2, 1, tq), F32), pltpu.VMEM((2, 128, tq), F32), pltpu.VMEM((2, tq, tq), F32),
                        pltpu.VMEM((d, 128), F32), pltpu.VMEM((FOX_HEADS, 128), F32),
                        pltpu.VMEM((FOX_HEADS, 128), F32), pltpu.VMEM((d, 128), F32),
                        pltpu.VMEM((FOX_HEADS, 128), F32)])
    att, out = pl.pallas_call(
        functools.partial(_fox_attn_paged_kernel, tq, npp),
        out_shape=(jax.ShapeDtypeStruct((bp, s, D_MODEL), BF16), jax.ShapeDtypeStruct((bs, 1, d), F32)),
        grid_spec=grid_spec, compiler_params=_cparams("arbitrary", "arbitrary", "arbitrary"), name="fox_attn_paged",
    )(page_table.reshape(-1), qa, ka, vat, q[:, None, :], k_new[:, None, :], v_new[:, None, :], lf_new[:, None, :],
      jnp.asarray(hm), jnp.asarray(ut, dtype=BF16),
      *([cache_kt] * npp), *([cache_vt] * npp), *([cache_lft] * npp))
    return att, out[:, 0, :]


def _router_weights(w_gc, b_gc, w_gf, b_gf):
    d = w_gc.shape[0]
    wf = jnp.transpose(w_gf, (1, 0, 2)).reshape(d, N_EXPERTS)
    wr = jnp.zeros((d, ROUTER_LANES), F32).at[:, :N_GROUPS].set(w_gc).at[:, N_GROUPS:N_GROUPS + N_EXPERTS].set(wf)
    br = jnp.zeros((1, ROUTER_LANES), F32).at[0, :N_GROUPS].set(b_gc).at[0, N_GROUPS:N_GROUPS + N_EXPERTS].set(
        b_gf.reshape(-1))
    hi = wr.astype(BF16)
    lo = (wr - hi.astype(F32)).astype(BF16)
    return hi, lo, br


def kernel(x_prompt, x_sample, mem_prompt, state_pool, state_ret, cache_fox_k, cache_fox_v, cache_fox_logf,
           cache_mem_k, cache_mem_v, page_table, ln_g, ln_b, ab_w_in, pool_w, pool_scale, ret_gn_g, ab_w_o,
           fox_w_in, fox_b_f, fox_w_o, mem_wq, mem_wkv, mem_wo, moe_w_gc, moe_b_gc, moe_w_gf, moe_b_gf,
           moe_w_up, moe_w_gate, moe_w_down):
    bp, s, d = x_prompt.shape
    bs = x_sample.shape[0]
    n_phys = cache_fox_k.shape[1]
    row = lambda a: a.reshape(1, -1)

    yp = x_prompt
    ys = x_sample.reshape(bs, d)
    new_mk_p, new_mv_p, mk_all, mv_all = _mem_kv(mem_prompt.reshape(bp * N_MEM, d), mem_wkv.astype(BF16), bp)

    perm = _qk_perm()
    inv_perm = np.argsort(perm)
    for l in range(DEPTH):
        g0, b0 = row(ln_g[l, 0]), row(ln_b[l, 0])
        if l % 2 == 0:
            e = l // 2
            cols = np.concatenate([np.arange(POOL_WIDTH), POOL_WIDTH + perm, POOL_WIDTH + RET_QK + perm,
                                   np.arange(POOL_WIDTH + 2 * RET_QK, ab_w_in.shape[-1])])
            w_in = ab_w_in[e][:, cols].astype(BF16)
            w_pool = pool_w[e].astype(BF16)
            w_o = ab_w_o[e].astype(BF16)
            yp, pool_p, ret_p = _ab_prompt(yp, w_in, w_pool, row(pool_scale[e]), row(ret_gn_g[e]), w_o, g0, b0)
            ret_prev = state_ret[e].reshape(bs, RET_QK, RET_V_DIM)[:, perm, :]
            ys, pool_s, ret_s = _ab_sample(ys, w_in, state_pool[e], ret_prev, w_pool, row(pool_scale[e]),
                                           row(ret_gn_g[e]), w_o, g0, b0)
            new_pool_p = pool_p[:, POOL_HIST - POOL_STATE:, :][None]
            new_pool_s = pool_s[None]
            rp = ret_p[:, inv_perm, :].reshape(bp, RET_HEADS, RET_QK_DIM, RET_HEADS, RET_V_DIM)
            new_ret_p = jnp.stack([rp[:, h, :, h, :] for h in range(RET_HEADS)], axis=1)[None]
            new_ret_s = ret_s[:, inv_perm, :].reshape(bs, RET_HEADS, RET_QK_DIM, RET_V_DIM)[None]
        else:
            o = l // 2
            w = fox_w_in[o]
            w_qkv = w[:, :3 * d].astype(BF16)
            wf = w[:, 3 * d:]
            w_f3 = jnp.pad(jnp.concatenate([wf, wf, wf], axis=1), ((0, 0), (0, 128 - 3 * FOX_HEADS))).astype(BF16)
            bf = fox_b_f[o]
            b_f3 = jnp.pad(jnp.concatenate([bf, bf, bf]), (0, 128 - 3 * FOX_HEADS)).reshape(1, 128)
            w_o = fox_w_o[o].astype(BF16)
            w_vt = jnp.transpose(w[:, 2 * d:3 * d]).astype(BF16)
            qa, ka, vat, k, v, lf = _fox_proj(yp, w_qkv, w_vt, w_f3, b_f3)
            qs, ks, vs, lfs = _fox_proj_sample(ys, w_qkv, w_f3, b_f3)
            kt = jnp.transpose(cache_fox_k[o], (0, 2, 3, 1)).reshape(n_phys, d, PAGE_SIZE)
            vt = jnp.transpose(cache_fox_v[o], (0, 2, 3, 1)).reshape(n_phys, d, PAGE_SIZE)
            lft = jnp.transpose(cache_fox_logf[o], (0, 2, 1))
            att, att_s = _fox_attn_paged(qa, ka, vat, qs, ks, vs, lfs, kt, vt, lft, page_table)
            yp = _proj_ln(att.reshape(bp * s, d), yp.reshape(bp * s, d), w_o, g0, b0).reshape(bp, s, d)
            new_fk_p = k.reshape(1, bp, s, FOX_HEADS, FOX_HEAD_DIM)
            new_fv_p = v.reshape(1, bp, s, FOX_HEADS, FOX_HEAD_DIM)
            new_fl_p = lf[None]
            ys = _proj_ln(att_s, ys, w_o, g0, b0)
            new_fk_s = ks.reshape(1, bs, 1, FOX_HEADS, FOX_HEAD_DIM)
            new_fv_s = vs.reshape(1, bs, 1, FOX_HEADS, FOX_HEAD_DIM)
            new_fl_s = lfs[:, :FOX_HEADS].reshape(1, bs, 1, FOX_HEADS)

        g1, b1 = row(ln_g[l, 1]), row(ln_b[l, 1])
        wq = mem_wq[l].astype(BF16)
        wo = mem_wo[l].astype(BF16)
        yp = _mem_attn(yp, wq, mk_all[l].reshape(bp, N_MEM, d), mv_all[l].reshape(bp, N_MEM, d), wo, g1, b1)
        ys = _mem_attn_sample(ys, wq, cache_mem_k, cache_mem_v, l, wo, g1, b1)

        g2, b2 = row(ln_g[l, 2]), row(ln_b[l, 2])
        wr_hi, wr_lo, br = _router_weights(moe_w_gc[l], moe_b_gc[l], moe_w_gf[l], moe_b_gf[l])
        wg, wu, wd = moe_w_gate[l].astype(BF16), moe_w_up[l].astype(BF16), moe_w_down[l].astype(BF16)
        yp = _moe_sparse(yp.reshape(bp * s, d), wr_hi, wr_lo, br, wg, wu, wd, g2, b2).reshape(bp, s, d)
        ys = _moe_dense(ys, wr_hi, wr_lo, br, wg, wu, wd, g2, b2)

    return (yp, ys.reshape(bs, 1, d), new_pool_p, new_pool_s, new_ret_p, new_ret_s, new_fk_p, new_fk_s,
            new_fv_p, new_fv_s, new_fl_p, new_fl_s, new_mk_p, new_mv_p)
```

```python
import functools

import numpy as np
import jax
import jax.numpy as jnp
from jax import lax
from jax.experimental import pallas as pl
from jax.experimental.pallas import tpu as pltpu

F32 = jnp.float32
BF16 = jnp.bfloat16

D_MODEL = 1024
DEPTH = 2
PAST_LEN = 8192
PAGE_SIZE = 128

POOL_WIDTH = 512
POOL_GROUP_DIM = 128
POOL_WINDOWS = (2, 4, 8, 16)
POOL_STATE = 15
POOL_HIST = 16

RET_HEADS = 4
RET_QK_DIM = 64
RET_V_DIM = 128
RET_QK = 256
RET_V = 512
RET_CHUNK = 256
ROPE_BASE = 10000.0
ROPE_HALF = RET_QK_DIM // 2

FOX_HEADS = 16
FOX_HEAD_DIM = 64

N_MEM = 256
MEM_HEADS = 4
MEM_HEAD_DIM = 256

N_GROUPS = 4
EXPERTS_PER_GROUP = 4
N_EXPERTS = 16
EXPERT_HIDDEN = 256
ROUTER_LANES = 128

DEEPNORM_ALPHA = (2 * DEPTH) ** 0.25
LN_EPS = 1e-5
LOG2E = 1.4426950408889634

VMEM_LIMIT = 56 * 1024 * 1024


def _cparams(*sem):
    return pltpu.CompilerParams(dimension_semantics=sem, vmem_limit_bytes=VMEM_LIMIT)


def _full(shape):
    n = len(shape)
    return pl.BlockSpec(shape, lambda *_: (0,) * n)


def _ln(z, g, b):
    mu = jnp.mean(z, axis=-1, keepdims=True)
    zc = z - mu
    var = jnp.mean(zc * zc, axis=-1, keepdims=True)
    return zc * lax.rsqrt(var + LN_EPS) * g + b


def _dot(a, b):
    return jnp.dot(a, b, preferred_element_type=F32)


def _dot_nt(a, b):
    return lax.dot_general(a, b, (((1,), (1,)), ((), ())), preferred_element_type=F32)


def _silu(x):
    return x * jax.nn.sigmoid(x)


def _split_bf16(x):
    hi = x.astype(BF16)
    lo = (x - hi.astype(F32)).astype(BF16)
    return hi, lo


def _split3(x):
    hi = x.astype(BF16)
    r1 = x - hi.astype(F32)
    mid = r1.astype(BF16)
    lo = (r1 - mid.astype(F32)).astype(BF16)
    return hi, mid, lo


def _qk_perm():
    p = np.arange(RET_QK)
    h = (p % 128) // ROPE_HALF
    d = (p % ROPE_HALF) + ROPE_HALF * (p // 128)
    return h * RET_QK_DIM + d


def _lane_head():
    return (np.arange(RET_QK) % 128) // ROPE_HALF


def _log_gamma():
    return np.log(1.0 - 2.0 ** (-5.0 - np.arange(RET_HEADS, dtype=np.float64)))


def _rope_tables(pos):
    inv = ROPE_BASE ** (-np.arange(ROPE_HALF, dtype=np.float64) / ROPE_HALF)
    ang = pos.astype(np.float64)[:, None] * inv[None, :]
    cos = np.tile(np.cos(ang).astype(np.float32), (1, RET_HEADS))
    sin = np.tile(np.sin(ang).astype(np.float32), (1, RET_HEADS))
    return cos, sin


def _ret_tables(chunk):
    lg = _log_gamma()
    lh = _lane_head()
    i = np.arange(chunk, dtype=np.float64)
    dist = i[:, None] - i[None, :]
    dmask = np.where(dist >= 0, np.exp(lg[:, None, None] * np.maximum(dist, 0.0)), 0.0)
    qdec = np.exp(lg[lh][None, :] * (i[:, None] + 1.0))
    kdec = np.exp(lg[lh][None, :] * (chunk - 1.0 - i[:, None]))
    sdec = np.repeat(np.exp(lg[lh] * chunk)[:, None], RET_V, axis=1)
    bmask = (lh[:, None] == (np.arange(RET_V) // RET_V_DIM)[None, :]).astype(np.float64)
    hmask = (lh[None, :] == np.arange(RET_HEADS)[:, None]).astype(np.float64)
    f = lambda a: jnp.asarray(a.astype(np.float32))
    return f(dmask), f(qdec), f(kdec), f(sdec), f(bmask), f(hmask)


def _rotary_cat(x, cos, sin):
    x1, x2 = x[:, :128], x[:, 128:]
    return jnp.concatenate([x1 * cos - x2 * sin, x1 * sin + x2 * cos], axis=1)


def _head_norm(o, gn):
    parts = []
    for h in range(RET_HEADS):
        oh = o[:, h * RET_V_DIM:(h + 1) * RET_V_DIM]
        mu = jnp.mean(oh, axis=-1, keepdims=True)
        oc = oh - mu
        var = jnp.mean(oc * oc, axis=-1, keepdims=True)
        parts.append(oc * lax.rsqrt(var + LN_EPS))
    return jnp.concatenate(parts, axis=1) * gn


def _ab_prompt_kernel(tm, x_ref, win_ref, cos_ref, sin_ref, dmask_ref, qdec_ref, kdec_ref, sdec_ref, bmask_ref,
                      hmask_ref, wpool_ref, spool_ref, gn_ref, wo_ref, lng_ref, lnb_ref,
                      y_ref, pool_ref, ret_ref, uext, state, cat):
    j = pl.program_id(1)

    @pl.when(j == 0)
    def _():
        uext[0:POOL_HIST, :] = jnp.zeros((POOL_HIST, POOL_WIDTH), F32)
        state[...] = jnp.zeros(state.shape, F32)

    x = x_ref[0]
    h = _dot(x.astype(BF16), win_ref[...])
    u = h[:, :POOL_WIDTH]
    uext[POOL_HIST:POOL_HIST + tm, :] = u

    pos = j * tm + lax.broadcasted_iota(jnp.int32, (tm, 1), 0)
    for g, w in enumerate(POOL_WINDOWS):
        lanes = slice(g * POOL_GROUP_DIM, (g + 1) * POOL_GROUP_DIM)
        win = u[:, lanes]
        for s in range(1, w):
            win = win + uext[POOL_HIST - s:POOL_HIST - s + tm, lanes]
        cnt = jnp.minimum(w, pos + 1).astype(F32)
        diff = win / cnt - u[:, lanes]
        mixed = _dot(diff.astype(BF16), wpool_ref[g]) * spool_ref[:, lanes]
        cat[:, lanes] = mixed.astype(BF16)
    tail = uext[tm:tm + POOL_HIST, :]
    uext[0:POOL_HIST, :] = tail
    pool_ref[0] = tail

    cos = cos_ref[...]
    sin = sin_ref[...]
    qr = _rotary_cat(h[:, POOL_WIDTH:POOL_WIDTH + RET_QK], cos, sin)
    kr = _rotary_cat(h[:, POOL_WIDTH + RET_QK:POOL_WIDTH + 2 * RET_QK], cos, sin) * (RET_QK_DIM ** -0.5)
    v = h[:, POOL_WIDTH + 2 * RET_QK:POOL_WIDTH + 2 * RET_QK + RET_V]
    gate = h[:, POOL_WIDTH + 2 * RET_QK + RET_V:]
    gn = gn_ref[...]
    c = RET_CHUNK
    for ci in range(tm // c):
        rows = slice(ci * c, (ci + 1) * c)
        qc, kc = qr[rows], kr[rows]
        kcb = kc.astype(BF16)
        vcb = v[rows].astype(BF16)
        sm = state[...]
        o = _dot((qc * qdec_ref[...]).astype(BF16), sm.astype(BF16))
        parts = []
        for hh in range(RET_HEADS):
            qm = (qc * hmask_ref[hh:hh + 1, :]).astype(BF16)
            sc = _dot_nt(qm, kcb) * dmask_ref[hh]
            parts.append(_dot(sc.astype(BF16), vcb[:, hh * RET_V_DIM:(hh + 1) * RET_V_DIM]))
        o = o + jnp.concatenate(parts, axis=1)
        kdt = (kc * kdec_ref[...]).T.astype(BF16)
        state[...] = sm * sdec_ref[...] + _dot(kdt, vcb) * bmask_ref[...]
        cat[rows, POOL_WIDTH:] = (_silu(gate[rows]) * _head_norm(o, gn)).astype(BF16)
    ret_ref[0] = state[...]

    mix = _dot(cat[...], wo_ref[...])
    y_ref[0] = _ln(DEEPNORM_ALPHA * x + mix, lng_ref[...], lnb_ref[...])


def _ab_prompt(x, w_in, w_pool, s_pool, gn_g, w_o, ln_g, ln_b):
    b, s, d = x.shape
    tm = min(512, s)
    cos, sin = _rope_tables(np.arange(s))
    tabs = _ret_tables(RET_CHUNK)
    nin = w_in.shape[1]
    in_specs = [
        pl.BlockSpec((1, tm, d), lambda bi, j: (bi, j, 0)),
        _full((d, nin)),
        pl.BlockSpec((tm, 128), lambda bi, j: (j, 0)),
        pl.BlockSpec((tm, 128), lambda bi, j: (j, 0)),
    ] + [_full(t.shape) for t in tabs] + [
        _full(w_pool.shape), _full(s_pool.shape), _full(gn_g.shape), _full(w_o.shape),
        _full(ln_g.shape), _full(ln_b.shape),
    ]
    out_shape = (jax.ShapeDtypeStruct((b, s, d), F32),
                 jax.ShapeDtypeStruct((b, POOL_HIST, POOL_WIDTH), F32),
                 jax.ShapeDtypeStruct((b, RET_QK, RET_V), F32))
    out_specs = (pl.BlockSpec((1, tm, d), lambda bi, j: (bi, j, 0)),
                 pl.BlockSpec((1, POOL_HIST, POOL_WIDTH), lambda bi, j: (bi, 0, 0)),
                 pl.BlockSpec((1, RET_QK, RET_V), lambda bi, j: (bi, 0, 0)))
    return pl.pallas_call(
        functools.partial(_ab_prompt_kernel, tm),
        out_shape=out_shape, grid=(b, s // tm), in_specs=in_specs, out_specs=out_specs,
        scratch_shapes=[pltpu.VMEM((POOL_HIST + tm, POOL_WIDTH), F32),
                        pltpu.VMEM((RET_QK, RET_V), F32),
                        pltpu.VMEM((tm, d), BF16)],
        compiler_params=_cparams("arbitrary", "arbitrary"), name="ab_prompt",
    )(x, w_in, jnp.asarray(cos), jnp.asarray(sin), *tabs, w_pool, s_pool, gn_g, w_o, ln_g, ln_b)


def _ab_sample_kernel(x_ref, win_ref, cos_ref, sin_ref, hmask_ref, gdec_ref, pool_ref, ret_ref,
                      wpool_ref, spool_ref, gn_ref, wo_ref, lng_ref, lnb_ref,
                      y_ref, npool_ref, nret_ref, h_s, cat_s):
    b = pl.program_id(0)
    nb = pl.num_programs(0)

    @pl.when(b == 0)
    def _():
        h_s[...] = _dot(x_ref[...].astype(BF16), win_ref[...])

    h = h_s[pl.ds(b, 1), :]
    u = h[:, :POOL_WIDTH]
    prev = pool_ref[0]
    npool_ref[0, 0:POOL_STATE - 1, :] = prev[1:POOL_STATE, :]
    npool_ref[0, POOL_STATE - 1:POOL_STATE, :] = u
    diffs = []
    for g, w in enumerate(POOL_WINDOWS):
        lanes = slice(g * POOL_GROUP_DIM, (g + 1) * POOL_GROUP_DIM)
        win = u[:, lanes] + jnp.sum(prev[POOL_STATE - (w - 1):, lanes], axis=0, keepdims=True)
        diffs.append(win / float(w) - u[:, lanes])

    cos = cos_ref[...]
    sin = sin_ref[...]
    qr = _rotary_cat(h[:, POOL_WIDTH:POOL_WIDTH + RET_QK], cos, sin)
    kr = _rotary_cat(h[:, POOL_WIDTH + RET_QK:POOL_WIDTH + 2 * RET_QK], cos, sin) * (RET_QK_DIM ** -0.5)
    v = h[:, POOL_WIDTH + 2 * RET_QK:POOL_WIDTH + 2 * RET_QK + RET_V]
    gate = h[:, POOL_WIDTH + 2 * RET_QK + RET_V:]
    hm = hmask_ref[...]
    gdec = gdec_ref[...]
    sp = ret_ref[0]
    q4 = qr * hm
    k4 = kr * hm
    sc = jnp.sum(q4 * kr, axis=-1, keepdims=True)
    v4 = jnp.concatenate([v[:, hh * RET_V_DIM:(hh + 1) * RET_V_DIM] for hh in range(RET_HEADS)]
                         + [jnp.zeros((8 - RET_HEADS, RET_V_DIM), F32)], axis=0)
    o4 = sc.astype(BF16).astype(F32) * v4.astype(BF16).astype(F32) \
        + _dot((q4 * gdec).astype(BF16), sp.astype(BF16))
    kv = lax.dot_general(k4.astype(BF16), v4.astype(BF16), (((0,), (0,)), ((), ())), preferred_element_type=F32)
    gcol = jnp.sum(hm * gdec, axis=0, keepdims=True)
    nret_ref[0] = sp * gcol.T + kv
    mu = jnp.mean(o4, axis=-1, keepdims=True)
    oc = o4 - mu
    var = jnp.mean(oc * oc, axis=-1, keepdims=True)
    on = oc * lax.rsqrt(var + LN_EPS)
    on_row = jnp.concatenate([on[hh:hh + 1, :] for hh in range(RET_HEADS)], axis=1) * gn_ref[...]
    cat_s[pl.ds(b, 1), :] = jnp.concatenate(diffs + [_silu(gate) * on_row], axis=1)

    @pl.when(b == nb - 1)
    def _():
        c = cat_s[...]
        mixed = [_dot(c[:, g * POOL_GROUP_DIM:(g + 1) * POOL_GROUP_DIM].astype(BF16), wpool_ref[g])
                 * spool_ref[:, g * POOL_GROUP_DIM:(g + 1) * POOL_GROUP_DIM] for g in range(len(POOL_WINDOWS))]
        cat = jnp.concatenate(mixed + [c[:, POOL_WIDTH:]], axis=1).astype(BF16)
        mix = _dot(cat, wo_ref[...])
        y_ref[...] = _ln(DEEPNORM_ALPHA * x_ref[...] + mix, lng_ref[...], lnb_ref[...])


def _ab_sample(x2d, w_in, pool_prev, ret_prev_perm, w_pool, s_pool, gn_g, w_o, ln_g, ln_b):
    b, d = x2d.shape
    cos, sin = _rope_tables(np.array([PAST_LEN]))
    hmask = np.zeros((8, RET_QK), np.float32)
    hmask[:RET_HEADS] = (_lane_head()[None, :] == np.arange(RET_HEADS)[:, None])
    gdec = np.exp(_log_gamma())[_lane_head()][None, :].astype(np.float32)
    per_b = lambda shape: pl.BlockSpec((1,) + shape, lambda i: (i, 0, 0))
    in_specs = [_full((b, d)), _full(w_in.shape), _full((1, 128)), _full((1, 128)), _full((8, RET_QK)),
                _full((1, RET_QK)), per_b((POOL_STATE, POOL_WIDTH)), per_b((RET_QK, RET_V_DIM)),
                _full(w_pool.shape), _full(s_pool.shape), _full(gn_g.shape), _full(w_o.shape),
                _full(ln_g.shape), _full(ln_b.shape)]
    out_shape = (jax.ShapeDtypeStruct((b, d), F32),
                 jax.ShapeDtypeStruct((b, POOL_STATE, POOL_WIDTH), F32),
                 jax.ShapeDtypeStruct((b, RET_QK, RET_V_DIM), F32))
    out_specs = (_full((b, d)), per_b((POOL_STATE, POOL_WIDTH)), per_b((RET_QK, RET_V_DIM)))
    return pl.pallas_call(
        _ab_sample_kernel, out_shape=out_shape, grid=(b,), in_specs=in_specs, out_specs=out_specs,
        scratch_shapes=[pltpu.VMEM((b, w_in.shape[1]), F32), pltpu.VMEM((b, d), F32)],
        compiler_params=_cparams("arbitrary"), name="ab_sample",
    )(x2d, w_in, jnp.asarray(cos), jnp.asarray(sin), jnp.asarray(hmask), jnp.asarray(gdec),
      pool_prev, ret_prev_perm, w_pool, s_pool, gn_g, w_o, ln_g, ln_b)


def _mem_kv_kernel(m_ref, w_ref, k_ref, v_ref, kb_ref, vb_ref):
    h = _dot(m_ref[...].astype(BF16), w_ref[0])
    kb_ref[0] = h[:, :D_MODEL].astype(BF16)
    vb_ref[0] = h[:, D_MODEL:].astype(BF16)
    bsz = k_ref.shape[1]
    for hh in range(MEM_HEADS):
        lanes = slice(hh * MEM_HEAD_DIM, (hh + 1) * MEM_HEAD_DIM)
        k_ref[0, :, :, hh, :] = h[:, lanes].reshape(bsz, N_MEM, MEM_HEAD_DIM)
        v_ref[0, :, :, hh, :] = h[:, D_MODEL + hh * MEM_HEAD_DIM:D_MODEL + (hh + 1) * MEM_HEAD_DIM].reshape(
            bsz, N_MEM, MEM_HEAD_DIM)


def _mem_kv(mem2d, w_kv, bsz):
    n, d = mem2d.shape
    depth = w_kv.shape[0]
    cache = jax.ShapeDtypeStruct((depth, bsz, N_MEM, MEM_HEADS, MEM_HEAD_DIM), F32)
    flat = jax.ShapeDtypeStruct((depth, n, d), BF16)
    cspec = pl.BlockSpec((1, bsz, N_MEM, MEM_HEADS, MEM_HEAD_DIM), lambda l: (l, 0, 0, 0, 0))
    fspec = pl.BlockSpec((1, n, d), lambda l: (l, 0, 0))
    return pl.pallas_call(
        _mem_kv_kernel, out_shape=(cache, cache, flat, flat), grid=(depth,),
        in_specs=[_full((n, d)), pl.BlockSpec((1, d, 2 * d), lambda l: (l, 0, 0))],
        out_specs=(cspec, cspec, fspec, fspec),
        compiler_params=_cparams("arbitrary"), name="mem_kv",
    )(mem2d, w_kv)


def _mem_attn_kernel(x_ref, wq_ref, k_ref, v_ref, wo_ref, lng_ref, lnb_ref, y_ref):
    x = x_ref[0]
    q = _dot(x.astype(BF16), wq_ref[...]) * (MEM_HEAD_DIM ** -0.5)
    parts = []
    for h in range(MEM_HEADS):
        lanes = slice(h * MEM_HEAD_DIM, (h + 1) * MEM_HEAD_DIM)
        logits = _dot_nt(q[:, lanes].astype(BF16), k_ref[0, :, lanes].astype(BF16))
        m = jnp.max(logits, axis=-1, keepdims=True)
        p = jnp.exp(logits - m)
        l = jnp.sum(p, axis=-1, keepdims=True)
        parts.append(_dot(p.astype(BF16), v_ref[0, :, lanes].astype(BF16)) / l)
    o = jnp.concatenate(parts, axis=1)
    mix = _dot(o.astype(BF16), wo_ref[...])
    y_ref[0] = _ln(DEEPNORM_ALPHA * x + mix, lng_ref[...], lnb_ref[...])


MEM_TILE = 1024


def _mem_attn(x, w_q, k, v, w_o, ln_g, ln_b):
    b, s, d = x.shape
    tm = min(MEM_TILE, s)
    xs = pl.BlockSpec((1, tm, d), lambda bi, j: (bi, j, 0))
    kvs = pl.BlockSpec((1, N_MEM, d), lambda bi, j: (bi, 0, 0))
    return pl.pallas_call(
        _mem_attn_kernel, out_shape=jax.ShapeDtypeStruct((b, s, d), F32), grid=(b, s // tm),
        in_specs=[xs, _full(w_q.shape), kvs, kvs, _full(w_o.shape), _full(ln_g.shape), _full(ln_b.shape)],
        out_specs=xs, compiler_params=_cparams("arbitrary", "arbitrary"), name="mem_attn",
    )(x, w_q, k, v, w_o, ln_g, ln_b)


def _mem_attn_sample_kernel(x_ref, wq_ref, k_ref, v_ref, wo_ref, lng_ref, lnb_ref, y_ref, q_s, o_s):
    b = pl.program_id(0)
    nb = pl.num_programs(0)

    @pl.when(b == 0)
    def _():
        q_s[...] = _dot(x_ref[...].astype(BF16), wq_ref[...]) * (MEM_HEAD_DIM ** -0.5)

    qrow = q_s[pl.ds(b, 1), :]
    q4 = jnp.concatenate([qrow[:, h * MEM_HEAD_DIM:(h + 1) * MEM_HEAD_DIM] for h in range(MEM_HEADS)]
                         + [jnp.zeros((8 - MEM_HEADS, MEM_HEAD_DIM), F32)], axis=0)
    k2 = k_ref[0, 0].reshape(N_MEM * MEM_HEADS, MEM_HEAD_DIM).astype(BF16)
    v2 = v_ref[0, 0].reshape(N_MEM * MEM_HEADS, MEM_HEAD_DIM).astype(BF16)
    logits = _dot_nt(q4.astype(BF16), k2)
    col = lax.broadcasted_iota(jnp.int32, logits.shape, 1)
    rowi = lax.broadcasted_iota(jnp.int32, logits.shape, 0)
    own = (col % MEM_HEADS) == rowi
    m = jnp.max(jnp.where(own, logits, -jnp.inf), axis=-1, keepdims=True)
    m = jnp.where(rowi[:, :1] < MEM_HEADS, m, 0.0)
    p = jnp.where(own, jnp.exp(logits - m), 0.0)
    l = jnp.maximum(jnp.sum(p, axis=-1, keepdims=True), jnp.where(rowi[:, :1] < MEM_HEADS, 0.0, 1.0))
    o4 = _dot(p.astype(BF16), v2) / l
    o_s[pl.ds(b, 1), :] = jnp.concatenate([o4[h:h + 1, :] for h in range(MEM_HEADS)], axis=1)

    @pl.when(b == nb - 1)
    def _():
        mix = _dot(o_s[...].astype(BF16), wo_ref[...])
        y_ref[...] = _ln(DEEPNORM_ALPHA * x_ref[...] + mix, lng_ref[...], lnb_ref[...])


def _mem_attn_sample(x2d, w_q, cache_k, cache_v, layer, w_o, ln_g, ln_b):
    n, d = x2d.shape
    kvs = pl.BlockSpec((1, 1, N_MEM, MEM_HEADS, MEM_HEAD_DIM), lambda b: (layer, b, 0, 0, 0))
    return pl.pallas_call(
        _mem_attn_sample_kernel, out_shape=jax.ShapeDtypeStruct((n, d), F32), grid=(n,),
        in_specs=[_full((n, d)), _full(w_q.shape), kvs, kvs, _full(w_o.shape), _full(ln_g.shape), _full(ln_b.shape)],
        out_specs=_full((n, d)),
        scratch_shapes=[pltpu.VMEM((n, d), F32), pltpu.VMEM((n, d), F32)],
        compiler_params=_cparams("arbitrary"), name="mem_attn_sample",
    )(x2d, w_q, cache_k, cache_v, w_o, ln_g, ln_b)


def _router_gates(x, wr_hi, wr_lo, br):
    xh, xl = _split_bf16(x)
    lg = _dot(xh, wr_hi) + (_dot(xh, wr_lo) + _dot(xl, wr_hi)) + br
    n = x.shape[0]
    lane_i = lax.broadcasted_iota(jnp.int32, (n, ROUTER_LANES), 1)
    lane = lane_i.astype(F32)
    is_c = lane_i < N_GROUPS
    neg = jnp.float32(-jnp.inf)
    big = jnp.float32(ROUTER_LANES)
    coarse = jnp.where(is_c, lg, neg)
    cmax = jnp.max(coarse, axis=-1, keepdims=True)
    grp = jnp.min(jnp.where(coarse == cmax, lane, big), axis=-1, keepdims=True)
    p_grp = 1.0 / jnp.sum(jnp.where(is_c, jnp.exp(lg - cmax), 0.0), axis=-1, keepdims=True)
    lane_grp = ((lane_i - N_GROUPS) // EXPERTS_PER_GROUP).astype(F32)
    in_grp = (lane_i >= N_GROUPS) & (lane_i < N_GROUPS + N_EXPERTS) & (lane_grp == grp)
    f1 = jnp.where(in_grp, lg, neg)
    t1 = jnp.max(f1, axis=-1, keepdims=True)
    i1 = jnp.min(jnp.where(f1 == t1, lane, big), axis=-1, keepdims=True)
    f2 = jnp.where(lane == i1, neg, f1)
    t2 = jnp.max(f2, axis=-1, keepdims=True)
    i2 = jnp.min(jnp.where(f2 == t2, lane, big), axis=-1, keepdims=True)
    e2 = jnp.exp(t2 - t1)
    w1 = p_grp / (1.0 + e2)
    w2 = p_grp * e2 / (1.0 + e2)
    return jnp.where(lane == i1, w1, 0.0) + jnp.where(lane == i2, w2, 0.0) + jnp.where(lane == grp, 1.0, 0.0)


MOE_TILE = 1024
MOE_SUB = 512
MOE_CHUNK = 160


def _moe_sparse_kernel(x_ref, wrh_ref, wrl_ref, br_ref, tri_ref, wg_ref, wu_ref, wd_ref, lng_ref, lnb_ref, y_ref,
                       xb, gates3, onehot, rank, cnt, acc):
    g = pl.program_id(1)
    tm = x_ref.shape[0]
    nsub = tm // MOE_SUB
    lane = lax.broadcasted_iota(jnp.int32, (MOE_SUB, ROUTER_LANES), 1)

    @pl.when(g == 0)
    def _():
        x = x_ref[...]
        xb[...] = x.astype(BF16)
        gt = _router_gates(x, wrh_ref[...], wrl_ref[...], br_ref[...])
        hi, mid, lo = _split3(gt)
        gates3[0] = hi
        gates3[1] = mid
        gates3[2] = lo
        lane_t = lax.broadcasted_iota(jnp.int32, gt.shape, 1)
        oh = jnp.where(lane_t < N_GROUPS, gt, 0.0).astype(BF16)
        onehot[...] = oh
        cnt[...] = jnp.zeros(cnt.shape, F32)
        for st in range(nsub):
            rows = slice(st * MOE_SUB, (st + 1) * MOE_SUB)
            pref = _dot(tri_ref[...], oh[rows])
            rank[rows, :] = pref - 1.0
            cnt[st:st + 1, :] = pref[MOE_SUB - 1:, :]
        acc[...] = jnp.zeros(acc.shape, F32)

    lane_c = lax.broadcasted_iota(jnp.int32, (MOE_SUB, MOE_CHUNK), 1).astype(F32)
    lane_cnt = lax.broadcasted_iota(jnp.int32, cnt.shape, 1)
    most = jnp.max(jnp.where(lane_cnt == g, cnt[...], 0.0))
    n_pass = ((most + float(MOE_CHUNK - 1)) / float(MOE_CHUNK)).astype(jnp.int32)

    def one_pass(c, carry):
        first = (c * MOE_CHUNK).astype(F32)
        sels, xs, gs = [], [], []
        for st in range(nsub):
            rows = slice(st * MOE_SUB, (st + 1) * MOE_SUB)
            member = jnp.sum(jnp.where(lane == g, onehot[rows, :].astype(F32), 0.0), axis=-1, keepdims=True)
            rk = jnp.sum(jnp.where(lane == g, rank[rows, :], 0.0), axis=-1, keepdims=True) - first
            sel = jnp.where((member > 0.0) & (rk == lane_c), 1.0, 0.0).astype(BF16)
            sels.append(sel)
            tn = lambda a, sel=sel: lax.dot_general(sel, a, (((0,), (0,)), ((), ())), preferred_element_type=F32)
            xs.append(tn(xb[rows, :]).astype(BF16))
            gs.append(tn(gates3[0, rows, :]) + (tn(gates3[1, rows, :]) + tn(gates3[2, rows, :])))
        xg = jnp.concatenate(xs, axis=0)
        gg = jnp.concatenate(gs, axis=0)
        lane_g = lax.broadcasted_iota(jnp.int32, gg.shape, 1)
        y = jnp.zeros((nsub * MOE_CHUNK, x_ref.shape[1]), F32)
        for e in range(EXPERTS_PER_GROUP):
            gcol = jnp.sum(jnp.where(lane_g == N_GROUPS + EXPERTS_PER_GROUP * g + e, gg, 0.0),
                           axis=-1, keepdims=True)
            hid = _silu(_dot(xg, wg_ref[0, e])) * _dot(xg, wu_ref[0, e]) * gcol
            y = y + _dot(hid.astype(BF16), wd_ref[0, e])
        yh, yl = _split_bf16(y)
        for st in range(nsub):
            rows = slice(st * MOE_SUB, (st + 1) * MOE_SUB)
            part = slice(st * MOE_CHUNK, (st + 1) * MOE_CHUNK)
            acc[rows, :] += _dot(sels[st], yh[part]) + _dot(sels[st], yl[part])
        return carry

    lax.fori_loop(0, n_pass, one_pass, 0)

    @pl.when(g == N_GROUPS - 1)
    def _():
        y_ref[...] = _ln(DEEPNORM_ALPHA * x_ref[...] + acc[...], lng_ref[...], lnb_ref[...])


def _moe_sparse(x2d, wr_hi, wr_lo, br, w_gate, w_up, w_down, ln_g, ln_b):
    n, d = x2d.shape
    tm = MOE_TILE
    assert n % tm == 0 and tm % MOE_SUB == 0
    tri = jnp.asarray(np.tril(np.ones((MOE_SUB, MOE_SUB), np.float32)), dtype=BF16)
    grouped = lambda w: w.reshape(N_GROUPS, EXPERTS_PER_GROUP, w.shape[1], w.shape[2])
    xs = pl.BlockSpec((tm, d), lambda i, g: (i, 0))
    wspec = lambda a, b: pl.BlockSpec((1, EXPERTS_PER_GROUP, a, b), lambda i, g: (g, 0, 0, 0))
    return pl.pallas_call(
        _moe_sparse_kernel, out_shape=jax.ShapeDtypeStruct((n, d), F32), grid=(n // tm, N_GROUPS),
        in_specs=[xs, _full(wr_hi.shape), _full(wr_lo.shape), _full(br.shape), _full(tri.shape),
                  wspec(d, EXPERT_HIDDEN), wspec(d, EXPERT_HIDDEN), wspec(EXPERT_HIDDEN, d),
                  _full(ln_g.shape), _full(ln_b.shape)],
        out_specs=xs,
        scratch_shapes=[pltpu.VMEM((tm, d), BF16), pltpu.VMEM((3, tm, ROUTER_LANES), BF16),
                        pltpu.VMEM((tm, ROUTER_LANES), BF16), pltpu.VMEM((tm, ROUTER_LANES), F32),
                        pltpu.VMEM((8, ROUTER_LANES), F32), pltpu.VMEM((tm, d), F32)],
        compiler_params=_cparams("arbitrary", "arbitrary"), name="moe_sparse",
    )(x2d, wr_hi, wr_lo, br, tri, grouped(w_gate), grouped(w_up), grouped(w_down), ln_g, ln_b)


def _moe_dense_kernel(x_ref, wrh_ref, wrl_ref, br_ref, wg_ref, wu_ref, wd_ref, lng_ref, lnb_ref, y_ref,
                      xb, gates, acc):
    e = pl.program_id(1)

    @pl.when(e == 0)
    def _():
        x = x_ref[...]
        xb[...] = x.astype(BF16)
        gates[...] = _router_gates(x, wrh_ref[...], wrl_ref[...], br_ref[...])
        acc[...] = jnp.zeros(acc.shape, F32)

    lane_i = lax.broadcasted_iota(jnp.int32, gates.shape, 1)
    gcol = jnp.sum(jnp.where(lane_i == e + N_GROUPS, gates[...], 0.0), axis=-1, keepdims=True)
    xv = xb[...]
    hid = _silu(_dot(xv, wg_ref[0])) * _dot(xv, wu_ref[0]) * gcol
    acc[...] += _dot(hid.astype(BF16), wd_ref[0])

    @pl.when(e == N_EXPERTS - 1)
    def _():
        y_ref[...] = _ln(DEEPNORM_ALPHA * x_ref[...] + acc[...], lng_ref[...], lnb_ref[...])


def _moe_dense(x2d, wr_hi, wr_lo, br, w_gate, w_up, w_down, ln_g, ln_b):
    n, d = x2d.shape
    tm = min(1024, n)
    xs = pl.BlockSpec((tm, d), lambda i, e: (i, 0))
    return pl.pallas_call(
        _moe_dense_kernel, out_shape=jax.ShapeDtypeStruct((n, d), F32), grid=(n // tm, N_EXPERTS),
        in_specs=[xs, _full(wr_hi.shape), _full(wr_lo.shape), _full(br.shape),
                  pl.BlockSpec((1, d, EXPERT_HIDDEN), lambda i, e: (e, 0, 0)),
                  pl.BlockSpec((1, d, EXPERT_HIDDEN), lambda i, e: (e, 0, 0)),
                  pl.BlockSpec((1, EXPERT_HIDDEN, d), lambda i, e: (e, 0, 0)),
                  _full(ln_g.shape), _full(ln_b.shape)],
        out_specs=xs,
        scratch_shapes=[pltpu.VMEM((tm, d), BF16), pltpu.VMEM((tm, ROUTER_LANES), F32), pltpu.VMEM((tm, d), F32)],
        compiler_params=_cparams("arbitrary", "arbitrary"), name="moe_dense",
    )(x2d, wr_hi, wr_lo, br, w_gate, w_up, w_down, ln_g, ln_b)


def _log_sigmoid(z):
    return jnp.minimum(z, 0.0) - jnp.log1p(jnp.exp(-jnp.abs(z)))


FOX_AUG = 2 * FOX_HEADS * FOX_HEAD_DIM
FOX_TILE = 512


def _fox_aug_tables():
    pq = np.zeros((128, FOX_AUG), np.float32)
    pk = np.zeros((128, FOX_AUG), np.float32)
    cq = np.zeros((1, FOX_AUG), np.float32)
    ck = np.zeros((1, FOX_AUG), np.float32)
    cv = np.zeros((1, FOX_AUG), np.float32)
    dm = np.zeros((1, FOX_AUG), np.float32)
    for h in range(FOX_HEADS):
        base = 128 * h
        off = base + (FOX_HEAD_DIM if h % 2 == 0 else 0)
        data = base + (0 if h % 2 == 0 else FOX_HEAD_DIM)
        dm[0, data:data + FOX_HEAD_DIM] = 1.0
        for g in range(3):
            pq[16 * g + h, off + g] = 1.0
            cq[0, off + 3 + g] = 1.0
            ck[0, off + g] = 1.0
            pk[16 * g + h, off + 3 + g] = -1.0
        cv[0, off] = 1.0
    return pq, pk, cq, ck, cv.T.copy(), dm, dm.T.copy()


def _fox_proj_kernel(x_ref, w_ref, wvt_ref, wf_ref, bf_ref, tri_ref, pq_ref, pk_ref, cq_ref, ck_ref, cvt_ref,
                     dm_ref, dmt_ref, qa_ref, ka_ref, vat_ref, k_ref, v_ref, lf_ref, carry):
    j = pl.program_id(1)

    @pl.when(j == 0)
    def _():
        carry[...] = jnp.zeros(carry.shape, F32)

    xb = x_ref[0].astype(BF16)
    h = _dot(xb, w_ref[...])
    q2 = h[:, :D_MODEL] * (FOX_HEAD_DIM ** -0.5 * LOG2E)
    k = h[:, D_MODEL:2 * D_MODEL]
    k_ref[0] = k
    v_ref[0] = h[:, 2 * D_MODEL:]
    vt = _dot_nt(wvt_ref[...], xb)
    lf = _log_sigmoid(_dot(xb, wf_ref[...]) + bf_ref[...])
    lf_ref[0] = lf[:, :FOX_HEADS]
    hi, mid, lo = _split3(lf)
    tri = tri_ref[...]
    c = _dot(tri, hi) + (_dot(tri, mid) + _dot(tri, lo)) + carry[...]
    carry[...] = c[c.shape[0] - 1:, :]
    chi, cmid, clo = _split3(c * LOG2E)
    lane = lax.broadcasted_iota(jnp.int32, c.shape, 1)
    c3 = jnp.where(lane < 16, chi.astype(F32), jnp.where(lane < 32, cmid.astype(F32), clo.astype(F32))).astype(BF16)
    qb = _dot(c3, pq_ref[...]) + cq_ref[...]
    kb = _dot(c3, pk_ref[...]) + ck_ref[...]
    for hh in range(FOX_HEADS):
        blk = slice(128 * hh, 128 * hh + 128)
        src = slice(128 * (hh // 2), 128 * (hh // 2) + 128)
        is_data = dm_ref[:, blk] > 0.0
        qa_ref[0, :, blk] = jnp.where(is_data, q2[:, src], qb[:, blk]).astype(BF16)
        ka_ref[0, :, blk] = jnp.where(is_data, k[:, src], kb[:, blk]).astype(BF16)
        vat_ref[0, blk, :] = jnp.where(dmt_ref[blk, :] > 0.0, vt[src, :], cvt_ref[blk, :]).astype(BF16)


def _fox_proj(x, w_qkv, w_vt, w_f3, b_f3):
    b, s, d = x.shape
    tm = min(FOX_TILE, s)
    tri = jnp.asarray(np.tril(np.ones((tm, tm), np.float32)), dtype=BF16)
    pq, pk, cq, ck, cvt, dm, dmt = _fox_aug_tables()
    consts = [jnp.asarray(pq, dtype=BF16), jnp.asarray(pk, dtype=BF16), jnp.asarray(cq), jnp.asarray(ck),
              jnp.asarray(cvt), jnp.asarray(dm), jnp.asarray(dmt)]
    xs = pl.BlockSpec((1, tm, d), lambda bi, j: (bi, j, 0))
    augs = pl.BlockSpec((1, tm, FOX_AUG), lambda bi, j: (bi, j, 0))
    f32o = jax.ShapeDtypeStruct((b, s, d), F32)
    augo = jax.ShapeDtypeStruct((b, s, FOX_AUG), BF16)
    return pl.pallas_call(
        _fox_proj_kernel,
        out_shape=(augo, augo, jax.ShapeDtypeStruct((b, FOX_AUG, s), BF16), f32o, f32o,
                   jax.ShapeDtypeStruct((b, s, FOX_HEADS), F32)),
        grid=(b, s // tm),
        in_specs=[xs, _full(w_qkv.shape), _full(w_vt.shape), _full(w_f3.shape), _full(b_f3.shape), _full(tri.shape)]
        + [_full(c.shape) for c in consts],
        out_specs=(augs, augs, pl.BlockSpec((1, FOX_AUG, tm), lambda bi, j: (bi, 0, j)), xs, xs,
                   pl.BlockSpec((1, tm, FOX_HEADS), lambda bi, j: (bi, j, 0))),
        scratch_shapes=[pltpu.VMEM((1, 128), F32)],
        compiler_params=_cparams("arbitrary", "arbitrary"), name="fox_proj",
    )(x, w_qkv, w_vt, w_f3, b_f3, tri, *consts)


def _fox_attn_body(tq, i, q_ref, k_ref, vt_ref, o_ref, m_s, acc_s, s_buf):
    krow = lax.broadcasted_iota(jnp.int32, (tq, tq), 0)
    qcol = lax.broadcasted_iota(jnp.int32, (tq, tq), 1)
    m_s[...] = jnp.full(m_s.shape, -jnp.inf, F32)
    acc_s[...] = jnp.zeros(acc_s.shape, F32)
    heads = [slice(128 * hd, 128 * hd + 128) for hd in range(2)]

    def scores(jj, diagonal):
        k0 = pl.multiple_of(jj * tq, tq)
        sts = [_dot_nt(k_ref[0, pl.ds(k0, tq), lanes], q_ref[0, :, lanes]) for lanes in heads]
        if diagonal:
            sts = [jnp.where(krow <= qcol, st, -jnp.inf) for st in sts]
        return sts

    def consume(jj):
        k0 = pl.multiple_of(jj * tq, tq)
        for hd, lanes in enumerate(heads):
            st = s_buf[hd]
            m_old = m_s[hd]
            m_new = jnp.maximum(m_old, jnp.max(st, axis=0, keepdims=True))
            pt = jnp.exp2(st - m_new).astype(BF16)
            acc_s[hd] = jnp.exp2(m_old - m_new) * acc_s[hd] + _dot(vt_ref[0, lanes, pl.ds(k0, tq)], pt)
            m_s[hd] = m_new

    def step(jj, diagonal_next):
        nxt = scores(jj + 1, diagonal_next)
        consume(jj)
        for hd in range(2):
            s_buf[hd] = nxt[hd]

    @pl.when(i == 0)
    def _():
        for hd, st in enumerate(scores(0, True)):
            s_buf[hd] = st

    @pl.when(i > 0)
    def _():
        for hd, st in enumerate(scores(0, False)):
            s_buf[hd] = st

        def body(jj, carry):
            step(jj, False)
            return carry

        lax.fori_loop(0, i - 1, body, 0)
        step(i - 1, True)

    consume(i)
    outs = []
    for hd in range(2):
        acc = acc_s[hd]
        l_row = FOX_HEAD_DIM if hd == 0 else 0
        outs.append(acc / acc[l_row:l_row + 1, :])
    row = lax.broadcasted_iota(jnp.int32, (128, tq), 0)
    o_ref[0] = jnp.where(row < FOX_HEAD_DIM, outs[0], outs[1]).T.astype(BF16)


def _proj_ln_kernel(o_ref, x_ref, w_ref, lng_ref, lnb_ref, y_ref):
    mix = _dot(o_ref[...].astype(BF16), w_ref[...])
    y_ref[...] = _ln(DEEPNORM_ALPHA * x_ref[...] + mix, lng_ref[...], lnb_ref[...])


def _proj_ln(o2d, x2d, w, ln_g, ln_b):
    n, d = x2d.shape
    tm = min(1024, n)
    xs = pl.BlockSpec((tm, d), lambda i: (i, 0))
    return pl.pallas_call(
        _proj_ln_kernel, out_shape=jax.ShapeDtypeStruct((n, d), F32), grid=(n // tm,),
        in_specs=[xs, xs, _full(w.shape), _full(ln_g.shape), _full(ln_b.shape)], out_specs=xs,
        compiler_params=_cparams("arbitrary"), name="proj_ln",
    )(o2d, x2d, w, ln_g, ln_b)


def _fox_proj_sample_kernel(x_ref, w_ref, wf_ref, bf_ref, q_ref, k_ref, v_ref, lf_ref):
    xb = x_ref[...].astype(BF16)
    h = _dot(xb, w_ref[...])
    q_ref[...] = h[:, :D_MODEL] * (FOX_HEAD_DIM ** -0.5)
    k_ref[...] = h[:, D_MODEL:2 * D_MODEL]
    v_ref[...] = h[:, 2 * D_MODEL:]
    lf_ref[...] = _log_sigmoid(_dot(xb, wf_ref[...]) + bf_ref[...])


def _fox_proj_sample(x2d, w_qkv, w_f, b_f):
    n, d = x2d.shape
    o = jax.ShapeDtypeStruct((n, d), F32)
    return pl.pallas_call(
        _fox_proj_sample_kernel, out_shape=(o, o, o, jax.ShapeDtypeStruct((n, 128), F32)),
        name="fox_proj_sample", compiler_params=pltpu.CompilerParams(vmem_limit_bytes=VMEM_LIMIT),
    )(x2d, w_qkv, w_f, b_f)


PAGES_PER_STEP = 8


def _fox_paged_body(s, ns, q_ref, kn_ref, vn_ref, lfn_ref, hm_ref, ut_ref, k_refs, v_refs, lf_refs, o_ref,
                    qb, m_s, l_s, acc_s, ccar):
    npp = len(k_refs)
    nh, hd = FOX_HEADS, FOX_HEAD_DIM

    @pl.when(s == 0)
    def _():
        qb[...] = jnp.broadcast_to(q_ref[0], (128, D_MODEL)).T
        m_s[...] = jnp.full(m_s.shape, -jnp.inf, F32)
        l_s[...] = jnp.zeros(l_s.shape, F32)
        acc_s[...] = jnp.zeros(acc_s.shape, F32)
        ccar[...] = jnp.zeros(ccar.shape, F32)

    lf = jnp.concatenate([r[0] for r in lf_refs], axis=0)
    hi, mid, lo = _split3(lf)
    ut = ut_ref[...]
    cs_tot = _dot(hi, ut) + (_dot(mid, ut) + _dot(lo, ut))
    base = ccar[...]
    svals = []
    partial = [[None] * nh for _ in range(npp)]
    for h in range(nh):
        hrows = slice(h * hd, (h + 1) * hd)
        qh = qb[hrows, :]
        for p in range(npp):
            partial[p][h] = jnp.sum((k_refs[p][0, hrows, :] * qh).reshape(hd // 8, 8, 128), axis=0)
    for p in range(npp):
        rows = slice(p * nh, (p + 1) * nh)
        c_p = cs_tot[rows, :128] + base
        base = base + cs_tot[rows, 128:]
        qk = jnp.sum(jnp.concatenate(partial[p], axis=0).reshape(nh, 8, 128), axis=1)
        svals.append(qk - c_p)
    ccar[...] = base

    smax = svals[0]
    for p in range(1, npp):
        smax = jnp.maximum(smax, svals[p])
    m_old = m_s[...]
    m_new = jnp.maximum(m_old, jnp.max(smax, axis=-1, keepdims=True))
    alpha = jnp.exp(m_old - m_new)
    ps = [jnp.exp(sv - m_new) for sv in svals]
    psum = ps[0]
    for p in range(1, npp):
        psum = psum + ps[p]
    l_s[...] = alpha * l_s[...] + jnp.sum(psum, axis=-1, keepdims=True)
    m_s[...] = m_new

    def rows64(x):
        return jnp.concatenate([jnp.broadcast_to(x[h:h + 1, :], (hd, 128)) for h in range(nh)], axis=0)

    for h in range(nh):
        hrows = slice(h * hd, (h + 1) * hd)
        a = acc_s[hrows, :] * jnp.broadcast_to(alpha[h:h + 1, :], (hd, 128))
        for p in range(npp):
            a = a + jnp.broadcast_to(ps[p][h:h + 1, :], (hd, 128)) * v_refs[p][0, hrows, :]
        acc_s[hrows, :] = a

    @pl.when(s == ns - 1)
    def _():
        hm = hm_ref[...]
        lane = lax.broadcasted_iota(jnp.int32, (nh, 128), 1)
        head = lax.broadcasted_iota(jnp.int32, (nh, 128), 0)
        lf_col = jnp.sum(jnp.where(lane == head, jnp.broadcast_to(lfn_ref[0], (nh, 128)), 0.0),
                         axis=-1, keepdims=True)
        qk = jnp.broadcast_to(q_ref[0] * kn_ref[0], (nh, D_MODEL)) * hm
        s_new = jnp.sum(qk, axis=-1, keepdims=True) - (ccar[...] + lf_col)
        m_prev = m_s[...]
        m_fin = jnp.maximum(m_prev, s_new)
        a2 = jnp.exp(m_prev - m_fin)
        pn = jnp.exp(s_new - m_fin)
        l_fin = a2 * l_s[...] + pn
        vnb = jnp.broadcast_to(vn_ref[0], (128, D_MODEL)).T
        lane_w = lax.broadcasted_iota(jnp.int32, (D_MODEL, 128), 1)
        tot = acc_s[...] * rows64(a2) + jnp.where(lane_w == 0, rows64(pn) * vnb, 0.0)
        o_col = jnp.sum(tot, axis=-1, keepdims=True) / rows64(l_fin)
        o_ref[0] = o_col.T[0:1, :]


def _fox_attn_paged_kernel(tq, npp, pt_ref, qa_ref, ka_ref, vat_ref, q_ref, kn_ref, vn_ref, lfn_ref, hm_ref, ut_ref,
                           *rest):
    del pt_ref
    k_refs, v_refs, lf_refs = rest[:npp], rest[npp:2 * npp], rest[2 * npp:3 * npp]
    o_att, o_smp = rest[3 * npp:3 * npp + 2]
    m_a, acc_a, s_buf, qb, m_p, l_p, acc_p, ccar = rest[3 * npp + 2:]
    i = pl.program_id(2)
    _fox_attn_body(tq, i, qa_ref, ka_ref, vat_ref, o_att, m_a, acc_a, s_buf)
    _fox_paged_body(i, pl.num_programs(2), q_ref, kn_ref, vn_ref, lfn_ref, hm_ref, ut_ref, k_refs, v_refs, lf_refs,
                    o_smp, qb, m_p, l_p, acc_p, ccar)


def _fox_attn_paged(qa, ka, vat, q, k_new, v_new, lf_new, cache_kt, cache_vt, cache_lft, page_table):
    bp, s, _ = qa.shape
    bs, d = q.shape
    tq = min(FOX_TILE, s)
    nq = s // tq
    n_pages = page_table.shape[1]
    npp = PAGES_PER_STEP
    hm = np.zeros((FOX_HEADS, d), np.float32)
    for h in range(FOX_HEADS):
        hm[h, h * FOX_HEAD_DIM:(h + 1) * FOX_HEAD_DIM] = 1.0
    ut = np.concatenate([np.triu(np.ones((128, 128), np.float32)), np.ones((128, 128), np.float32)], axis=1)
    assert bs == bp * (FOX_HEADS // 2) and n_pages == npp * nq, (bs, bp, n_pages, nq)
    seq = lambda bi, hp: bi * (FOX_HEADS // 2) + hp
    row = lambda w: pl.BlockSpec((1, 1, w), lambda bi, hp, i, pt: (seq(bi, hp), 0, 0))
    const = lambda shape: pl.BlockSpec(shape, lambda bi, hp, i, pt: (0,) * len(shape))

    def page_spec(p, r):
        return pl.BlockSpec((1, r, PAGE_SIZE),
                            lambda bi, hp, i, pt: (pt[seq(bi, hp) * n_pages + i * npp + p], 0, 0))

    qs = pl.BlockSpec((1, tq, 256), lambda bi, hp, i, pt: (bi, i, hp))
    in_specs = ([qs, pl.BlockSpec((1, s, 256), lambda bi, hp, i, pt: (bi, 0, hp)),
                 pl.BlockSpec((1, 256, s), lambda bi, hp, i, pt: (bi, hp, 0)),
                 row(d), row(d), row(d), row(128), const((FOX_HEADS, d)), const((128, 256))]
                + [page_spec(p, d) for p in range(npp)] + [page_spec(p, d) for p in range(npp)]
                + [page_spec(p, FOX_HEADS) for p in range(npp)])
    grid_spec = pltpu.PrefetchScalarGridSpec(
        num_scalar_prefetch=1, grid=(bp, FOX_HEADS // 2, nq), in_specs=in_specs,
        out_specs=(pl.BlockSpec((1, tq, 128), lambda bi, hp, i, pt: (bi, i, hp)), row(d)),
        scratch_shapes=[pltpu.VMEM((2, 1, tq), F32), pltpu.VMEM((2, 128, tq), F32), pltpu.VMEM((2, tq, tq), F32),
                        pltpu.VMEM((d, 128), F32), pltpu.VMEM((FOX_HEADS, 128), F32),
                        pltpu.VMEM((FOX_HEADS, 128), F32), pltpu.VMEM((d, 128), F32),
                        pltpu.VMEM((FOX_HEADS, 128), F32)])
    att, out = pl.pallas_call(
        functools.partial(_fox_attn_paged_kernel, tq, npp),
        out_shape=(jax.ShapeDtypeStruct((bp, s, D_MODEL), BF16), jax.ShapeDtypeStruct((bs, 1, d), F32)),
        grid_spec=grid_spec, compiler_params=_cparams("arbitrary", "arbitrary", "arbitrary"), name="fox_attn_paged",
    )(page_table.reshape(-1), qa, ka, vat, q[:, None, :], k_new[:, None, :], v_new[:, None, :], lf_new[:, None, :],
      jnp.asarray(hm), jnp.asarray(ut, dtype=BF16),
      *([cache_kt] * npp), *([cache_vt] * npp), *([cache_lft] * npp))
    return att, out[:, 0, :]


def _router_weights(w_gc, b_gc, w_gf, b_gf):
    d = w_gc.shape[0]
    wf = jnp.transpose(w_gf, (1, 0, 2)).reshape(d, N_EXPERTS)
    wr = jnp.zeros((d, ROUTER_LANES), F32).at[:, :N_GROUPS].set(w_gc).at[:, N_GROUPS:N_GROUPS + N_EXPERTS].set(wf)
    br = jnp.zeros((1, ROUTER_LANES), F32).at[0, :N_GROUPS].set(b_gc).at[0, N_GROUPS:N_GROUPS + N_EXPERTS].set(
        b_gf.reshape(-1))
    hi = wr.astype(BF16)
    lo = (wr - hi.astype(F32)).astype(BF16)
    return hi, lo, br


def kernel(x_prompt, x_sample, mem_prompt, state_pool, state_ret, cache_fox_k, cache_fox_v, cache_fox_logf,
           cache_mem_k, cache_mem_v, page_table, ln_g, ln_b, ab_w_in, pool_w, pool_scale, ret_gn_g, ab_w_o,
           fox_w_in, fox_b_f, fox_w_o, mem_wq, mem_wkv, mem_wo, moe_w_gc, moe_b_gc, moe_w_gf, moe_b_gf,
           moe_w_up, moe_w_gate, moe_w_down):
    bp, s, d = x_prompt.shape
    bs = x_sample.shape[0]
    n_phys = cache_fox_k.shape[1]
    row = lambda a: a.reshape(1, -1)

    yp = x_prompt
    ys = x_sample.reshape(bs, d)
    new_mk_p, new_mv_p, mk_all, mv_all = _mem_kv(mem_prompt.reshape(bp * N_MEM, d), mem_wkv.astype(BF16), bp)

    perm = _qk_perm()
    inv_perm = np.argsort(perm)
    for l in range(DEPTH):
        g0, b0 = row(ln_g[l, 0]), row(ln_b[l, 0])
        if l % 2 == 0:
            e = l // 2
            cols = np.concatenate([np.arange(POOL_WIDTH), POOL_WIDTH + perm, POOL_WIDTH + RET_QK + perm,
                                   np.arange(POOL_WIDTH + 2 * RET_QK, ab_w_in.shape[-1])])
            w_in = ab_w_in[e][:, cols].astype(BF16)
            w_pool = pool_w[e].astype(BF16)
            w_o = ab_w_o[e].astype(BF16)
            yp, pool_p, ret_p = _ab_prompt(yp, w_in, w_pool, row(pool_scale[e]), row(ret_gn_g[e]), w_o, g0, b0)
            ret_prev = state_ret[e].reshape(bs, RET_QK, RET_V_DIM)[:, perm, :]
            ys, pool_s, ret_s = _ab_sample(ys, w_in, state_pool[e], ret_prev, w_pool, row(pool_scale[e]),
                                           row(ret_gn_g[e]), w_o, g0, b0)
            new_pool_p = pool_p[:, POOL_HIST - POOL_STATE:, :][None]
            new_pool_s = pool_s[None]
            rp = ret_p[:, inv_perm, :].reshape(bp, RET_HEADS, RET_QK_DIM, RET_HEADS, RET_V_DIM)
            new_ret_p = jnp.stack([rp[:, h, :, h, :] for h in range(RET_HEADS)], axis=1)[None]
            new_ret_s = ret_s[:, inv_perm, :].reshape(bs, RET_HEADS, RET_QK_DIM, RET_V_DIM)[None]
        else:
            o = l // 2
            w = fox_w_in[o]
            w_qkv = w[:, :3 * d].astype(BF16)
            wf = w[:, 3 * d:]
            w_f3 = jnp.pad(jnp.concatenate([wf, wf, wf], axis=1), ((0, 0), (0, 128 - 3 * FOX_HEADS))).astype(BF16)
            bf = fox_b_f[o]
            b_f3 = jnp.pad(jnp.concatenate([bf, bf, bf]), (0, 128 - 3 * FOX_HEADS)).reshape(1, 128)
            w_o = fox_w_o[o].astype(BF16)
            w_vt = jnp.transpose(w[:, 2 * d:3 * d]).astype(BF16)
            qa, ka, vat, k, v, lf = _fox_proj(yp, w_qkv, w_vt, w_f3, b_f3)
            qs, ks, vs, lfs = _fox_proj_sample(ys, w_qkv, w_f3, b_f3)
            kt = jnp.transpose(cache_fox_k[o], (0, 2, 3, 1)).reshape(n_phys, d, PAGE_SIZE)
            vt = jnp.transpose(cache_fox_v[o], (0, 2, 3, 1)).reshape(n_phys, d, PAGE_SIZE)
            lft = jnp.transpose(cache_fox_logf[o], (0, 2, 1))
            att, att_s = _fox_attn_paged(qa, ka, vat, qs, ks, vs, lfs, kt, vt, lft, page_table)
            yp = _proj_ln(att.reshape(bp * s, d), yp.reshape(bp * s, d), w_o, g0, b0).reshape(bp, s, d)
            new_fk_p = k.reshape(1, bp, s, FOX_HEADS, FOX_HEAD_DIM)
            new_fv_p = v.reshape(1, bp, s, FOX_HEADS, FOX_HEAD_DIM)
            new_fl_p = lf[None]
            ys = _proj_ln(att_s, ys, w_o, g0, b0)
            new_fk_s = ks.reshape(1, bs, 1, FOX_HEADS, FOX_HEAD_DIM)
            new_fv_s = vs.reshape(1, bs, 1, FOX_HEADS, FOX_HEAD_DIM)
            new_fl_s = lfs[:, :FOX_HEADS].reshape(1, bs, 1, FOX_HEADS)

        g1, b1 = row(ln_g[l, 1]), row(ln_b[l, 1])
        wq = mem_wq[l].astype(BF16)
        wo = mem_wo[l].astype(BF16)
        yp = _mem_attn(yp, wq, mk_all[l].reshape(bp, N_MEM, d), mv_all[l].reshape(bp, N_MEM, d), wo, g1, b1)
        ys = _mem_attn_sample(ys, wq, cache_mem_k, cache_mem_v, l, wo, g1, b1)

        g2, b2 = row(ln_g[l, 2]), row(ln_b[l, 2])
        wr_hi, wr_lo, br = _router_weights(moe_w_gc[l], moe_b_gc[l], moe_w_gf[l], moe_b_gf[l])
        wg, wu, wd = moe_w_gate[l].astype(BF16), moe_w_up[l].astype(BF16), moe_w_down[l].astype(BF16)
        yp = _moe_sparse(yp.reshape(bp * s, d), wr_hi, wr_lo, br, wg, wu, wd, g2, b2).reshape(bp, s, d)
        ys = _moe_dense(ys, wr_hi, wr_lo, br, wg, wu, wd, g2, b2)

    return (yp, ys.reshape(bs, 1, d), new_pool_p, new_pool_s, new_ret_p, new_ret_s, new_fk_p, new_fk_s,
            new_fv_p, new_fv_s, new_fl_p, new_fl_s, new_mk_p, new_mv_p)
```
